```python
import math
import jax, jax.numpy as jnp
from jax import lax
import numpy as np

D_MODEL = 2048
BATCH = 1
SEQ = 16384
DEPTH = 1
DEC_BATCH = 2
DEC_SEQ = 8192
PAST_LEN = 128

DA_HEADS = 8
DA_QK_DIM = 64
DA_V_DIM = 2 * DA_QK_DIM
DA_ROT = DA_QK_DIM // 4
DA_Q_W = DA_HEADS * 2 * DA_QK_DIM
DA_V_W = DA_HEADS * DA_V_DIM
MLA_HEADS = 8
MLA_NOPE = 128
MLA_ROPE = 64
MLA_V = 128
MLA_Q_RANK = 512
MLA_KV_RANK = 256
ROPE_THETA = 500000.0
D_FF = 5632
Q_BLOCK = 128
EPS = 1e-6
N_MOD = 9
IN_COLS = 2 * DA_Q_W + DA_V_W + MLA_Q_RANK + MLA_KV_RANK + MLA_ROPE
MIX_W = DA_V_W + MLA_HEADS * MLA_V

kernel_name = "hymba_diffattn_mla_macaron_adaln_encoder"


def rms_norm(x, g):
    xf = x.astype(jnp.float32)
    y = xf * lax.rsqrt(jnp.mean(xf * xf, axis=-1, keepdims=True) + EPS)
    return (y * g.astype(jnp.float32)).astype(x.dtype)


def rope_tables(seq, dim):
    inv = ROPE_THETA ** (-jnp.arange(0, dim, 2, dtype=jnp.float32) / dim)
    ang = jnp.arange(seq, dtype=jnp.float32)[:, None] * inv[None, :]
    return jnp.cos(ang), jnp.sin(ang)


def apply_rope(x, cos, sin):
    half = x.shape[-1] // 2
    shp = (cos.shape[0],) + (1,) * (x.ndim - 3) + (half,)
    c = cos.reshape(shp).astype(x.dtype)
    s = sin.reshape(shp).astype(x.dtype)
    x1, x2 = x[..., :half], x[..., half:]
    return jnp.concatenate([x1 * c - x2 * s, x1 * s + x2 * c], axis=-1)


def blocked_queries(fn, *qs):
    B, S = qs[0].shape[:2]
    nb = S // Q_BLOCK
    qb = tuple(jnp.moveaxis(q.reshape((B, nb, Q_BLOCK) + q.shape[2:]), 1, 0) for q in qs)
    out = lax.map(lambda a: fn(*a), qb)
    return jnp.moveaxis(out, 0, 1).reshape((B, S) + out.shape[3:])


def diff_attention(q, k, v, lam, subln_g, lambda_init):
    scale = DA_QK_DIM ** -0.5
    kf = k.astype(jnp.float32)
    vf = v.astype(jnp.float32)

    def block(qb):
        s = jnp.einsum('bqhjd,bkhjd->bhjqk', qb.astype(jnp.float32) * scale, kf)
        p = jax.nn.softmax(s, axis=-1)
        a = p[:, :, 0] - lam * p[:, :, 1]
        return jnp.einsum('bhqk,bkhd->bqhd', a, vf)

    o = blocked_queries(block, q)
    o = rms_norm(o, subln_g) * (1.0 - lambda_init)
    return o.astype(v.dtype)


def mla_attention(c_q, c_kv, k_pe, q_norm_g, kv_norm_g, w_uq, w_ukv, cos, sin):
    B, S, _ = c_q.shape
    q = (rms_norm(c_q, q_norm_g) @ w_uq).reshape(B, S, MLA_HEADS, MLA_NOPE + MLA_ROPE)
    q_nope = q[..., :MLA_NOPE]
    q_pe = apply_rope(q[..., MLA_NOPE:], cos, sin)
    kv = (rms_norm(c_kv, kv_norm_g) @ w_ukv).reshape(B, S, MLA_HEADS, MLA_NOPE + MLA_V)
    k_nope = kv[..., :MLA_NOPE].astype(jnp.float32)
    vf = kv[..., MLA_NOPE:].astype(jnp.float32)
    kpe = apply_rope(k_pe[:, :, None, :], cos, sin)[:, :, 0].astype(jnp.float32)
    scale = (MLA_NOPE + MLA_ROPE) ** -0.5

    def block(qn, qp):
        s = (jnp.einsum('bqhd,bkhd->bhqk', qn.astype(jnp.float32), k_nope)
             + jnp.einsum('bqhd,bkd->bhqk', qp.astype(jnp.float32), kpe))
        p = jax.nn.softmax(s * scale, axis=-1)
        return jnp.einsum('bhqk,bkhd->bqhd', p, vf)

    o = blocked_queries(block, q_nope, q_pe)
    return o.astype(c_q.dtype)


def swiglu(h, w1, w3, w2):
    return (jax.nn.silu(h @ w1) * (h @ w3)) @ w2


def modulate(x, g, shift, scale):
    return rms_norm(x, g) * (1.0 + scale) + shift


def encoder(x, c, ffn1_norm, ffn1_w1, ffn1_w3, ffn1_w2, attn_norm, w_in,
            da_lambda_q1, da_lambda_k1, da_lambda_q2, da_lambda_k2, da_subln,
            mla_q_norm, mla_w_uq, mla_kv_norm, mla_w_ukv, w_o,
            ffn2_norm, ffn2_w1, ffn2_w3, ffn2_w2, w_ada, b_ada, final_norm):
    B, S, _ = x.shape
    cos_da, sin_da = rope_tables(S, DA_ROT)
    cos_mla, sin_mla = rope_tables(S, MLA_ROPE)
    for l in range(DEPTH):
        m = jax.nn.silu(c) @ w_ada[l] + b_ada[l]
        sh1, sc1, g1, sh2, sc2, g2, sh3, sc3, g3 = jnp.split(m[:, None, :], N_MOD, axis=-1)

        h = modulate(x, ffn1_norm[l], sh1, sc1)
        x = x + 0.5 * g1 * swiglu(h, ffn1_w1[l], ffn1_w3[l], ffn1_w2[l])

        h = modulate(x, attn_norm[l], sh2, sc2)
        proj = h @ w_in[l]
        o1 = DA_Q_W
        o2 = o1 + DA_Q_W
        o3 = o2 + DA_V_W
        o4 = o3 + MLA_Q_RANK
        o5 = o4 + MLA_KV_RANK
        q_da = proj[..., :o1].reshape(B, S, DA_HEADS, 2, DA_QK_DIM)
        k_da = proj[..., o1:o2].reshape(B, S, DA_HEADS, 2, DA_QK_DIM)
        v_da = proj[..., o2:o3].reshape(B, S, DA_HEADS, DA_V_DIM)
        c_q = proj[..., o3:o4]
        c_kv = proj[..., o4:o5]
        k_pe = proj[..., o5:]

        q_da = jnp.concatenate([apply_rope(q_da[..., :DA_ROT], cos_da, sin_da), q_da[..., DA_ROT:]], axis=-1)
        k_da = jnp.concatenate([apply_rope(k_da[..., :DA_ROT], cos_da, sin_da), k_da[..., DA_ROT:]], axis=-1)
        lambda_init = 0.8 - 0.6 * math.exp(-0.3 * l)
        lam = (jnp.exp(jnp.sum(da_lambda_q1[l].astype(jnp.float32) * da_lambda_k1[l].astype(jnp.float32)))
               - jnp.exp(jnp.sum(da_lambda_q2[l].astype(jnp.float32) * da_lambda_k2[l].astype(jnp.float32)))
               + lambda_init)
        o_da = diff_attention(q_da, k_da, v_da, lam, da_subln[l], lambda_init)
        o_mla = mla_attention(c_q, c_kv, k_pe, mla_q_norm[l], mla_kv_norm[l],
                              mla_w_uq[l], mla_w_ukv[l], cos_mla, sin_mla)
        mixed = jnp.concatenate([o_da.reshape(B, S, DA_V_W), o_mla.reshape(B, S, MLA_HEADS * MLA_V)], axis=-1)
        x = x + g2 * (mixed @ w_o[l])

        h = modulate(x, ffn2_norm[l], sh3, sc3)
        x = x + 0.5 * g3 * swiglu(h, ffn2_w1[l], ffn2_w3[l], ffn2_w2[l])
    return rms_norm(x, final_norm)


def setup_inputs(seed: int = 0) -> dict:
    key = jax.random.key(seed)
    ks = iter(jax.random.split(key, 40))

    def w(shape, fan_in, mult=1.0):
        return jax.random.normal(next(ks), shape, jnp.float32) * (mult * fan_in ** -0.5)

    def gain(shape):
        return 1.0 + 0.01 * jax.random.normal(next(ks), shape, jnp.float32)

    def small(shape, s):
        return s * jax.random.normal(next(ks), shape, jnp.float32)

    L, D = DEPTH, D_MODEL
    return {
        "x_prompt": jax.random.normal(next(ks), (BATCH, SEQ, D), jnp.float32),
        "x_sample": jax.random.normal(next(ks), (DEC_BATCH, DEC_SEQ, D), jnp.float32),
        "c_prompt": jax.random.normal(next(ks), (BATCH, D), jnp.float32),
        "c_sample": jax.random.normal(next(ks), (DEC_BATCH, D), jnp.float32),
        "ffn1_norm": gain((L, D)),
        "ffn1_w1": w((L, D, D_FF), D),
        "ffn1_w3": w((L, D, D_FF), D),
        "ffn1_w2": w((L, D_FF, D), D_FF),
        "attn_norm": gain((L, D)),
        "w_in": w((L, D, IN_COLS), D),
        "da_lambda_q1": small((L, DA_QK_DIM), 0.1),
        "da_lambda_k1": small((L, DA_QK_DIM), 0.1),
        "da_lambda_q2": small((L, DA_QK_DIM), 0.1),
        "da_lambda_k2": small((L, DA_QK_DIM), 0.1),
        "da_subln": gain((L, DA_V_DIM)),
        "mla_q_norm": gain((L, MLA_Q_RANK)),
        "mla_w_uq": w((L, MLA_Q_RANK, MLA_HEADS * (MLA_NOPE + MLA_ROPE)), MLA_Q_RANK),
        "mla_kv_norm": gain((L, MLA_KV_RANK)),
        "mla_w_ukv": w((L, MLA_KV_RANK, MLA_HEADS * (MLA_NOPE + MLA_V)), MLA_KV_RANK),
        "w_o": w((L, MIX_W, D), MIX_W),
        "ffn2_norm": gain((L, D)),
        "ffn2_w1": w((L, D, D_FF), D),
        "ffn2_w3": w((L, D, D_FF), D),
        "ffn2_w2": w((L, D_FF, D), D_FF),
        "w_ada": w((L, D, N_MOD * D), D, 0.5),
        "b_ada": small((L, N_MOD * D), 0.02),
        "final_norm": gain((D,)),
    }


def reference(x_prompt, x_sample, c_prompt, c_sample, ffn1_norm, ffn1_w1, ffn1_w3, ffn1_w2,
              attn_norm, w_in, da_lambda_q1, da_lambda_k1, da_lambda_q2, da_lambda_k2, da_subln,
              mla_q_norm, mla_w_uq, mla_kv_norm, mla_w_ukv, w_o,
              ffn2_norm, ffn2_w1, ffn2_w3, ffn2_w2, w_ada, b_ada, final_norm):
    weights = (ffn1_norm, ffn1_w1, ffn1_w3, ffn1_w2, attn_norm, w_in,
               da_lambda_q1, da_lambda_k1, da_lambda_q2, da_lambda_k2, da_subln,
               mla_q_norm, mla_w_uq, mla_kv_norm, mla_w_ukv, w_o,
               ffn2_norm, ffn2_w1, ffn2_w3, ffn2_w2, w_ada, b_ada, final_norm)
    y_prompt = encoder(x_prompt, c_prompt, *weights)
    y_sample = encoder(x_sample, c_sample, *weights)
    return (y_prompt, y_sample)
```

```python
import functools
import math

import jax
import jax.numpy as jnp
import numpy as np
from jax import lax
from jax.experimental import pallas as pl
from jax.experimental.pallas import tpu as pltpu

D_MODEL = 2048
D_FF = 5632
N_MOD = 9
DA_HEADS = 8
DA_QK_DIM = 64
DA_V_DIM = 128
DA_ROT = 16
DA_Q_W = DA_HEADS * 2 * DA_QK_DIM
DA_V_W = DA_HEADS * DA_V_DIM
MLA_HEADS = 8
MLA_NOPE = 128
MLA_ROPE = 64
MLA_V = 128
MLA_Q_RANK = 512
MLA_KV_RANK = 256
ROPE_THETA = 500000.0
EPS = 1e-6
LANES = 128
MLA_QK_W = 2 * LANES
VMEM_LIMIT = 56 * 1024 * 1024
NEG_BIG = -1e30

FFN_TM, FFN_TF = 512, 512
PROJ_TM = 256
OUT_TM = 512
ATT_TQ, ATT_TK = 512, 512
ADA_TN = 1024

BF16 = jnp.bfloat16
F32 = jnp.float32


def _params(*sem):
    return pltpu.CompilerParams(dimension_semantics=sem, vmem_limit_bytes=VMEM_LIMIT)


def _rms(x, g):
    return x * lax.rsqrt(jnp.mean(x * x, axis=-1, keepdims=True) + EPS) * g


def _silu(x):
    return x / (1.0 + jnp.exp(-x))


def _const_spec(shape):
    zeros = (0,) * len(shape)
    return pl.BlockSpec(shape, lambda *_: zeros, pipeline_mode=pl.Buffered(1))


def _ada_body(c_ref, w_ref, b_ref, o_ref):
    a = _silu(c_ref[...]).astype(BF16)
    o_ref[...] = jnp.dot(a, w_ref[...].astype(BF16), preferred_element_type=F32) + b_ref[...]


def _ada(c_pad, w, b):
    rows, d = c_pad.shape
    n = w.shape[1]
    return pl.pallas_call(
        _ada_body,
        grid=(n // ADA_TN,),
        in_specs=[pl.BlockSpec((rows, d), lambda j: (0, 0)),
                  pl.BlockSpec((d, ADA_TN), lambda j: (0, j)),
                  pl.BlockSpec((1, ADA_TN), lambda j: (0, j))],
        out_specs=pl.BlockSpec((rows, ADA_TN), lambda j: (0, j)),
        out_shape=jax.ShapeDtypeStruct((rows, n), F32),
        compiler_params=_params("arbitrary"),
        name="ada",
    )(c_pad, w, b)


def _ffn_body(x_ref, mod_ref, g_ref, w1_ref, w3_ref, w2_ref, fg_ref, o_ref, h_ref, *, mod_base, final_norm):
    j = pl.program_id(1)

    @pl.when(j == 0)
    def _():
        shift = mod_ref[mod_base:mod_base + 1, :]
        scale = mod_ref[mod_base + 1:mod_base + 2, :]
        h = _rms(x_ref[...], g_ref[...]) * (1.0 + scale) + shift
        h_ref[...] = h.astype(BF16)
        o_ref[...] = jnp.zeros_like(o_ref)

    h = h_ref[...]
    a = jnp.dot(h, w1_ref[...], preferred_element_type=F32)
    b = jnp.dot(h, w3_ref[...], preferred_element_type=F32)
    u = (_silu(a) * b).astype(BF16)
    o_ref[...] += jnp.dot(u, w2_ref[...], preferred_element_type=F32)

    @pl.when(j == pl.num_programs(1) - 1)
    def _():
        gate = mod_ref[mod_base + 2:mod_base + 3, :]
        y = x_ref[...] + 0.5 * gate * o_ref[...]
        if final_norm:
            y = _rms(y, fg_ref[...])
        o_ref[...] = y


def _ffn(x, mod, g, w1, w3, w2, fg, *, seq, mod_base, final_norm):
    t, d = x.shape
    f = w1.shape[1]
    tm = min(FFN_TM, seq)
    tf = FFN_TF
    per_seq = seq // tm
    return pl.pallas_call(
        functools.partial(_ffn_body, mod_base=mod_base, final_norm=final_norm),
        grid=(t // tm, f // tf),
        in_specs=[pl.BlockSpec((tm, d), lambda i, j: (i, 0)),
                  pl.BlockSpec((None, N_MOD, d), lambda i, j: (i // per_seq, 0, 0)),
                  pl.BlockSpec((1, d), lambda i, j: (0, 0)),
                  pl.BlockSpec((d, tf), lambda i, j: (0, j)),
                  pl.BlockSpec((d, tf), lambda i, j: (0, j)),
                  pl.BlockSpec((tf, d), lambda i, j: (j, 0)),
                  pl.BlockSpec((1, d), lambda i, j: (0, 0))],
        out_specs=pl.BlockSpec((tm, d), lambda i, j: (i, 0)),
        out_shape=jax.ShapeDtypeStruct((t, d), F32),
        scratch_shapes=[pltpu.VMEM((tm, d), BF16)],
        compiler_params=_params("arbitrary", "arbitrary"),
        name="ffn_final" if final_norm else "ffn",
    )(x, mod, g, w1, w3, w2, fg)


def _rope(t, cos, sin_signed):
    return t * cos + pltpu.roll(t, LANES // 2, axis=1) * sin_signed


def _inproj_body(x_ref, mod_ref, g_ref, win_ref, cda_ref, sda_ref, cml_ref, sml_ref, m1_ref, m2_ref,
                 gq_ref, gkv_ref, wuq_ref, wukv_ref,
                 qda_ref, k1_ref, k2_ref, vtda_ref, qml_ref, kml_ref, vtml_ref):
    shift = mod_ref[3:4, :]
    scale = mod_ref[4:5, :]
    h = (_rms(x_ref[...], g_ref[...]) * (1.0 + scale) + shift).astype(BF16)
    cda, sda = cda_ref[...], sda_ref[...]
    cml, sml = cml_ref[...], sml_ref[...]
    mask1, mask2 = m1_ref[...], m2_ref[...]
    da_scale = DA_QK_DIM ** -0.5
    mla_scale = (MLA_NOPE + MLA_ROPE) ** -0.5

    def tile(a, k):
        return a[:, k * LANES:(k + 1) * LANES]

    pq = jnp.dot(h, win_ref[:, 0:DA_Q_W], preferred_element_type=F32)
    for hd in range(DA_HEADS):
        qda_ref[hd] = (_rope(tile(pq, hd), cda, sda) * da_scale).astype(BF16)
    pk = jnp.dot(h, win_ref[:, DA_Q_W:2 * DA_Q_W], preferred_element_type=F32)
    for hd in range(DA_HEADS):
        r = _rope(tile(pk, hd), cda, sda)
        k1_ref[hd] = (r * mask1).astype(BF16)
        k2_ref[hd] = (r * mask2).astype(BF16)
    pv = jnp.dot(h, win_ref[:, 2 * DA_Q_W:2 * DA_Q_W + DA_V_W], preferred_element_type=F32)
    for hd in range(DA_HEADS):
        vtda_ref[hd] = tile(pv, hd).T.astype(BF16)

    o3 = 2 * DA_Q_W + DA_V_W
    pr = jnp.dot(h, win_ref[:, o3:], preferred_element_type=F32)
    cq = _rms(pr[:, 0:MLA_Q_RANK], gq_ref[...]).astype(BF16)
    ckv = _rms(pr[:, MLA_Q_RANK:MLA_Q_RANK + MLA_KV_RANK], gkv_ref[...]).astype(BF16)
    kpe = _rope(pr[:, MLA_Q_RANK + MLA_KV_RANK:], cml, sml).astype(BF16)

    qm = jnp.dot(cq, wuq_ref[...], preferred_element_type=F32)
    for hd in range(MLA_HEADS):
        qml_ref[hd, :, 0:LANES] = (tile(qm, 2 * hd) * mla_scale).astype(BF16)
        qml_ref[hd, :, LANES:] = (_rope(tile(qm, 2 * hd + 1), cml, sml) * mla_scale).astype(BF16)
    kv = jnp.dot(ckv, wukv_ref[...], preferred_element_type=F32)
    for hd in range(MLA_HEADS):
        kml_ref[hd, :, 0:LANES] = tile(kv, hd).astype(BF16)
        kml_ref[hd, :, LANES:] = kpe
        vtml_ref[hd] = tile(kv, MLA_HEADS + hd).T.astype(BF16)


def _inproj(x, mod, g, win, tabs, masks, gq, gkv, wuq, wukv, *, batch, seq):
    t, d = x.shape
    tm = min(PROJ_TM, seq)
    per_seq = seq // tm
    row = lambda i: (i, 0)
    tab = lambda i: (i % per_seq, 0)
    head_rows = lambda i: (i // per_seq, 0, i % per_seq, 0)
    head_cols = lambda i: (i // per_seq, 0, 0, i % per_seq)
    hq = lambda w: pl.BlockSpec((None, DA_HEADS, tm, w), head_rows)
    vt = pl.BlockSpec((None, DA_HEADS, LANES, tm), head_cols)
    sd = jax.ShapeDtypeStruct
    return pl.pallas_call(
        _inproj_body,
        grid=(t // tm,),
        in_specs=[pl.BlockSpec((tm, d), row),
                  pl.BlockSpec((None, N_MOD, d), lambda i: (i // per_seq, 0, 0)),
                  _const_spec((1, d)),
                  _const_spec(win.shape),
                  pl.BlockSpec((tm, LANES), tab), pl.BlockSpec((tm, LANES), tab),
                  pl.BlockSpec((tm, LANES), tab), pl.BlockSpec((tm, LANES), tab),
                  _const_spec((1, LANES)), _const_spec((1, LANES)),
                  _const_spec((1, MLA_Q_RANK)), _const_spec((1, MLA_KV_RANK)),
                  _const_spec(wuq.shape), _const_spec(wukv.shape)],
        out_specs=[hq(LANES), hq(LANES), hq(LANES), vt, hq(MLA_QK_W), hq(MLA_QK_W), vt],
        out_shape=[sd((batch, DA_HEADS, seq, LANES), BF16), sd((batch, DA_HEADS, seq, LANES), BF16),
                   sd((batch, DA_HEADS, seq, LANES), BF16), sd((batch, DA_HEADS, LANES, seq), BF16),
                   sd((batch, MLA_HEADS, seq, MLA_QK_W), BF16), sd((batch, MLA_HEADS, seq, MLA_QK_W), BF16),
                   sd((batch, MLA_HEADS, LANES, seq), BF16)],
        compiler_params=_params("arbitrary"),
        name="inproj",
    )(x, mod, g, win, *tabs, *masks, gq, gkv, wuq, wukv)


def _online_softmax_step(k, q, vt, m, l, acc_ref):
    s = lax.dot_general(k, q, (((1,), (1,)), ((), ())), preferred_element_type=F32)
    m_new = jnp.maximum(m, jnp.max(s, axis=0, keepdims=True))
    p = jnp.exp(s - m_new)
    alpha = jnp.exp(m - m_new)
    l_new = alpha * l + jnp.sum(p, axis=0, keepdims=True)
    acc_ref[...] = alpha * acc_ref[...] + jnp.dot(vt, p.astype(BF16), preferred_element_type=F32)
    return m_new, l_new


def _da_body(lam_ref, g_ref, q_ref, k1_ref, k2_ref, vt_ref, o_ref, acc1_ref, acc2_ref, *, tk, lambda_init):
    tq = q_ref.shape[0]
    q = q_ref[...]
    acc1_ref[...] = jnp.zeros_like(acc1_ref)
    acc2_ref[...] = jnp.zeros_like(acc2_ref)

    def step(i, carry):
        m1, l1, m2, l2 = carry
        ks = pl.ds(pl.multiple_of(i * tk, tk), tk)
        vt = vt_ref[:, ks]
        m1, l1 = _online_softmax_step(k1_ref[ks, :], q, vt, m1, l1, acc1_ref)
        m2, l2 = _online_softmax_step(k2_ref[ks, :], q, vt, m2, l2, acc2_ref)
        return m1, l1, m2, l2

    m0 = jnp.full((1, tq), NEG_BIG, F32)
    l0 = jnp.zeros((1, tq), F32)
    _, l1, _, l2 = lax.fori_loop(0, k1_ref.shape[0] // tk, step, (m0, l0, m0, l0))

    lq1, lk1, lq2, lk2 = (lam_ref[r:r + 1, :] for r in range(4))
    lam = (jnp.exp(jnp.sum(lq1 * lk1, axis=-1, keepdims=True))
           - jnp.exp(jnp.sum(lq2 * lk2, axis=-1, keepdims=True)) + lambda_init)
    ot = acc1_ref[...] / l1 - lam * (acc2_ref[...] / l2)
    ot = ot * lax.rsqrt(jnp.mean(ot * ot, axis=0, keepdims=True) + EPS)
    o_ref[...] = (ot.T * g_ref[...] * (1.0 - lambda_init)).astype(o_ref.dtype)


def _mla_body(q_ref, k_ref, vt_ref, o_ref, acc_ref, *, tk):
    tq = q_ref.shape[0]
    q = q_ref[...]
    acc_ref[...] = jnp.zeros_like(acc_ref)

    def step(i, carry):
        ks = pl.ds(pl.multiple_of(i * tk, tk), tk)
        return _online_softmax_step(k_ref[ks, :], q, vt_ref[:, ks], *carry, acc_ref)

    m0 = jnp.full((1, tq), NEG_BIG, F32)
    l0 = jnp.zeros((1, tq), F32)
    _, l = lax.fori_loop(0, k_ref.shape[0] // tk, step, (m0, l0))
    o_ref[...] = (acc_ref[...] / l).T.astype(o_ref.dtype)


def _attn_specs(batch, heads, seq, qk_w):
    tq = min(ATT_TQ, seq)
    tk = min(ATT_TK, seq)
    q_spec = pl.BlockSpec((None, None, tq, qk_w), lambda b, h, i: (b, h, i, 0))
    k_spec = pl.BlockSpec((None, None, seq, qk_w), lambda b, h, i: (b, h, 0, 0))
    vt_spec = pl.BlockSpec((None, None, LANES, seq), lambda b, h, i: (b, h, 0, 0))
    o_spec = pl.BlockSpec((None, tq, LANES), lambda b, h, i: (b, i, h))
    o_shape = jax.ShapeDtypeStruct((batch, seq, heads * LANES), BF16)
    return tq, tk, (batch, heads, seq // tq), q_spec, k_spec, vt_spec, o_spec, o_shape


def _da_attn(lam_vecs, subln_g, q, k1, k2, vt, *, lambda_init):
    batch, heads, seq, _ = q.shape
    tq, tk, grid, q_spec, k_spec, vt_spec, o_spec, o_shape = _attn_specs(batch, heads, seq, LANES)
    return pl.pallas_call(
        functools.partial(_da_body, tk=tk, lambda_init=lambda_init),
        grid=grid,
        in_specs=[pl.BlockSpec(lam_vecs.shape, lambda b, h, i: (0, 0)),
                  pl.BlockSpec((1, LANES), lambda b, h, i: (0, 0)),
                  q_spec, k_spec, k_spec, vt_spec],
        out_specs=o_spec,
        out_shape=o_shape,
        scratch_shapes=[pltpu.VMEM((LANES, tq), F32), pltpu.VMEM((LANES, tq), F32)],
        compiler_params=_params("arbitrary", "arbitrary", "arbitrary"),
        name="da_attn",
    )(lam_vecs, subln_g, q, k1, k2, vt)


def _mla_attn(q, k, vt):
    batch, heads, seq, qk_w = q.shape
    tq, tk, grid, q_spec, k_spec, vt_spec, o_spec, o_shape = _attn_specs(batch, heads, seq, qk_w)
    return pl.pallas_call(
        functools.partial(_mla_body, tk=tk),
        grid=grid,
        in_specs=[q_spec, k_spec, vt_spec],
        out_specs=o_spec,
        out_shape=o_shape,
        scratch_shapes=[pltpu.VMEM((LANES, tq), F32)],
        compiler_params=_params("arbitrary", "arbitrary", "arbitrary"),
        name="mla_attn",
    )(q, k, vt)


def _outproj_body(x_ref, mod_ref, oda_ref, oml_ref, wa_ref, wb_ref, o_ref):
    mix = (jnp.dot(oda_ref[...], wa_ref[...], preferred_element_type=F32)
           + jnp.dot(oml_ref[...], wb_ref[...], preferred_element_type=F32))
    o_ref[...] = x_ref[...] + mod_ref[5:6, :] * mix


def _outproj(x, mod, o_da, o_mla, wo_a, wo_b, *, seq):
    t, d = x.shape
    tm = min(OUT_TM, seq)
    per_seq = seq // tm
    row = lambda i: (i, 0)
    return pl.pallas_call(
        _outproj_body,
        grid=(t // tm,),
        in_specs=[pl.BlockSpec((tm, d), row),
                  pl.BlockSpec((None, N_MOD, d), lambda i: (i // per_seq, 0, 0)),
                  pl.BlockSpec((tm, o_da.shape[1]), row),
                  pl.BlockSpec((tm, o_mla.shape[1]), row),
                  _const_spec(wo_a.shape), _const_spec(wo_b.shape)],
        out_specs=pl.BlockSpec((tm, d), row),
        out_shape=jax.ShapeDtypeStruct((t, d), F32),
        compiler_params=_params("arbitrary"),
        name="outproj",
    )(x, mod, o_da, o_mla, wo_a, wo_b)


def _da_lane_layout():
    src = np.zeros(LANES, np.int32)
    rot = np.full(LANES, -1, np.int32)
    sign = np.zeros(LANES, np.float32)
    mask = np.zeros((2, LANES), np.float32)
    half = DA_ROT // 2
    for j in range(2):
        base = j * DA_QK_DIM
        for i in range(half):
            src[j * half + i] = base + i
            rot[j * half + i] = i
            sign[j * half + i] = -1.0
            src[64 + j * half + i] = base + half + i
            rot[64 + j * half + i] = i
            sign[64 + j * half + i] = 1.0
            mask[j, j * half + i] = mask[j, 64 + j * half + i] = 1.0
        plain = DA_QK_DIM - DA_ROT
        start = DA_ROT + j * 64
        for i in range(plain):
            src[start + i] = base + DA_ROT + i
            mask[j, start + i] = 1.0
    return src, rot, sign, mask


def _mla_lane_layout():
    src = np.full(LANES, -1, np.int32)
    rot = np.full(LANES, -1, np.int32)
    sign = np.zeros(LANES, np.float32)
    half = MLA_ROPE // 2
    for i in range(half):
        src[i], rot[i], sign[i] = i, i, -1.0
        src[64 + i], rot[64 + i], sign[64 + i] = half + i, i, 1.0
    return src, rot, sign


def _take_cols(w, src):
    cols = jnp.take(w, jnp.asarray(np.maximum(src, 0)), axis=1)
    return cols * jnp.asarray((src >= 0).astype(np.float32))


def _rope_tables(seq, dim, rot, sign):
    inv = ROPE_THETA ** (-jnp.arange(0, dim, 2, dtype=F32) / dim)
    ang = jnp.arange(seq, dtype=F32)[:, None] * inv[None, :]
    cos, sin = jnp.cos(ang), jnp.sin(ang)
    idx = jnp.asarray(np.maximum(rot, 0))
    on = jnp.asarray((rot >= 0).astype(np.float32))
    cos_t = jnp.take(cos, idx, axis=1) * on + (1.0 - on)
    sin_t = jnp.take(sin, idx, axis=1) * jnp.asarray(sign)
    return cos_t, sin_t


def _prep_layer(l, ffn1_w1, ffn1_w3, ffn1_w2, w_in, mla_w_uq, mla_w_ukv, w_o, ffn2_w1, ffn2_w3, ffn2_w2):
    da_src, _, _, _ = _da_lane_layout()
    ml_src, _, _ = _mla_lane_layout()
    head_src = np.concatenate([h * 2 * DA_QK_DIM + da_src for h in range(DA_HEADS)])
    o1, o2, o3 = DA_Q_W, 2 * DA_Q_W, 2 * DA_Q_W + DA_V_W
    o5 = o3 + MLA_Q_RANK + MLA_KV_RANK
    wi = w_in[l]
    win = jnp.concatenate([jnp.take(wi[:, :o1], jnp.asarray(head_src), axis=1),
                           jnp.take(wi[:, o1:o2], jnp.asarray(head_src), axis=1),
                           wi[:, o2:o5],
                           _take_cols(wi[:, o5:], ml_src)], axis=1).astype(BF16)
    per_q = MLA_NOPE + MLA_ROPE
    uq_src = np.concatenate([np.concatenate([h * per_q + np.arange(MLA_NOPE),
                                             np.where(ml_src >= 0, h * per_q + MLA_NOPE + ml_src, -1)])
                             for h in range(MLA_HEADS)])
    per_kv = MLA_NOPE + MLA_V
    ukv_src = np.concatenate([h * per_kv + np.arange(MLA_NOPE) for h in range(MLA_HEADS)]
                             + [h * per_kv + MLA_NOPE + np.arange(MLA_V) for h in range(MLA_HEADS)])
    return dict(
        f1=(ffn1_w1[l].astype(BF16), ffn1_w3[l].astype(BF16), ffn1_w2[l].astype(BF16)),
        f2=(ffn2_w1[l].astype(BF16), ffn2_w3[l].astype(BF16), ffn2_w2[l].astype(BF16)),
        win=win,
        wuq=_take_cols(mla_w_uq[l], uq_src).astype(BF16),
        wukv=jnp.take(mla_w_ukv[l], jnp.asarray(ukv_src), axis=1).astype(BF16),
        wo_a=w_o[l][:DA_V_W].astype(BF16),
        wo_b=w_o[l][DA_V_W:].astype(BF16),
    )


def _encode_group(x, mods, prepped, small, final_norm_g):
    batch, seq, d = x.shape
    xt = x.reshape(batch * seq, d)
    _, da_rot, da_sign, da_mask = _da_lane_layout()
    _, ml_rot, ml_sign = _mla_lane_layout()
    tabs = _rope_tables(seq, DA_ROT, da_rot, da_sign) + _rope_tables(seq, MLA_ROPE, ml_rot, ml_sign)
    masks = (jnp.asarray(da_mask[0:1]), jnp.asarray(da_mask[1:2]))
    depth = len(prepped)
    row = lambda v: v.reshape(1, -1)
    for l in range(depth):
        p, s, mod = prepped[l], small[l], mods[l]
        lambda_init = 0.8 - 0.6 * math.exp(-0.3 * l)
        xt = _ffn(xt, mod, row(s["ffn1_norm"]), *p["f1"], row(s["ffn1_norm"]),
                  seq=seq, mod_base=0, final_norm=False)
        qda, k1, k2, vtda, qml, kml, vtml = _inproj(
            xt, mod, row(s["attn_norm"]), p["win"], tabs, masks, row(s["mla_q_norm"]), row(s["mla_kv_norm"]),
            p["wuq"], p["wukv"], batch=batch, seq=seq)
        o_da = _da_attn(s["lam_vecs"], row(s["da_subln"]), qda, k1, k2, vtda, lambda_init=lambda_init)
        o_mla = _mla_attn(qml, kml, vtml)
        xt = _outproj(xt, mod, o_da.reshape(batch * seq, -1), o_mla.reshape(batch * seq, -1),
                      p["wo_a"], p["wo_b"], seq=seq)
        last = l == depth - 1
        xt = _ffn(xt, mod, row(s["ffn2_norm"]), *p["f2"], row(final_norm_g),
                  seq=seq, mod_base=6, final_norm=last)
    return xt.reshape(batch, seq, d)


def kernel(x_prompt, x_sample, c_prompt, c_sample, ffn1_norm, ffn1_w1, ffn1_w3, ffn1_w2, attn_norm, w_in,
           da_lambda_q1, da_lambda_k1, da_lambda_q2, da_lambda_k2, da_subln, mla_q_norm, mla_w_uq, mla_kv_norm,
           mla_w_ukv, w_o, ffn2_norm, ffn2_w1, ffn2_w3, ffn2_w2, w_ada, b_ada, final_norm):
    depth = w_in.shape[0]
    nb_p, nb_s = c_prompt.shape[0], c_sample.shape[0]
    c_all = jnp.concatenate([c_prompt, c_sample], axis=0)
    rows = -(-c_all.shape[0] // 8) * 8
    c_pad = jnp.pad(c_all, ((0, rows - c_all.shape[0]), (0, 0)))

    prepped, small, mods_p, mods_s = [], [], [], []
    for l in range(depth):
        prepped.append(_prep_layer(l, ffn1_w1, ffn1_w3, ffn1_w2, w_in, mla_w_uq, mla_w_ukv, w_o,
                                   ffn2_w1, ffn2_w3, ffn2_w2))
        small.append(dict(
            ffn1_norm=ffn1_norm[l], attn_norm=attn_norm[l], ffn2_norm=ffn2_norm[l], da_subln=da_subln[l],
            mla_q_norm=mla_q_norm[l], mla_kv_norm=mla_kv_norm[l],
            lam_vecs=jnp.stack([da_lambda_q1[l], da_lambda_k1[l], da_lambda_q2[l], da_lambda_k2[l]])))
        m = _ada(c_pad, w_ada[l], b_ada[l].reshape(1, -1))
        mods_p.append(m[:nb_p].reshape(nb_p, N_MOD, D_MODEL))
        mods_s.append(m[nb_p:nb_p + nb_s].reshape(nb_s, N_MOD, D_MODEL))

    y_prompt = _encode_group(x_prompt, mods_p, prepped, small, final_norm)
    y_sample = _encode_group(x_sample, mods_s, prepped, small, final_norm)
    return (y_prompt, y_sample)
```

```python
import functools
import math

import jax
import jax.numpy as jnp
import numpy as np
from jax import lax
from jax.experimental import pallas as pl
from jax.experimental.pallas import tpu as pltpu

D_MODEL = 2048
D_FF = 5632
N_MOD = 9
DA_HEADS = 8
DA_QK_DIM = 64
DA_V_DIM = 128
DA_ROT = 16
DA_Q_W = DA_HEADS * 2 * DA_QK_DIM
DA_V_W = DA_HEADS * DA_V_DIM
MLA_HEADS = 8
MLA_NOPE = 128
MLA_ROPE = 64
MLA_V = 128
MLA_Q_RANK = 512
MLA_KV_RANK = 256
ROPE_THETA = 500000.0
EPS = 1e-6
LANES = 128
MLA_QK_W = 2 * LANES
VMEM_LIMIT = 56 * 1024 * 1024
NEG_BIG = -1e30
LOG2E = math.log2(math.e)

FFN_TM, FFN_TF = 512, 512
PROJ_TM = 256
OUT_TM = 512
ATT_TQ, ATT_TK = 512, 1024
MLA_QSTREAMS = 2
ADA_TN = 1024

BF16 = jnp.bfloat16
F32 = jnp.float32


def _params(*sem):
    return pltpu.CompilerParams(dimension_semantics=sem, vmem_limit_bytes=VMEM_LIMIT)


def _rms(x, g):
    return x * lax.rsqrt(jnp.mean(x * x, axis=-1, keepdims=True) + EPS) * g


def _silu(x):
    return x / (1.0 + jnp.exp(-x))


def _const_spec(shape):
    zeros = (0,) * len(shape)
    return pl.BlockSpec(shape, lambda *_: zeros, pipeline_mode=pl.Buffered(1))


def _ada_body(c_ref, w_ref, b_ref, o_ref):
    a = _silu(c_ref[...]).astype(BF16)
    o_ref[...] = jnp.dot(a, w_ref[...].astype(BF16), preferred_element_type=F32) + b_ref[...]


def _ada(c_pad, w, b):
    rows, d = c_pad.shape
    n = w.shape[1]
    return pl.pallas_call(
        _ada_body,
        grid=(n // ADA_TN,),
        in_specs=[pl.BlockSpec((rows, d), lambda j: (0, 0)),
                  pl.BlockSpec((d, ADA_TN), lambda j: (0, j)),
                  pl.BlockSpec((1, ADA_TN), lambda j: (0, j))],
        out_specs=pl.BlockSpec((rows, ADA_TN), lambda j: (0, j)),
        out_shape=jax.ShapeDtypeStruct((rows, n), F32),
        compiler_params=_params("arbitrary"),
        name="ada",
    )(c_pad, w, b)


def _ffn_body(x_ref, mod_ref, g_ref, w1_ref, w3_ref, w2_ref, fg_ref, o_ref, h_ref, *, mod_base, final_norm):
    j = pl.program_id(1)

    @pl.when(j == 0)
    def _():
        shift = mod_ref[mod_base:mod_base + 1, :]
        scale = mod_ref[mod_base + 1:mod_base + 2, :]
        h = _rms(x_ref[...], g_ref[...]) * (1.0 + scale) + shift
        h_ref[...] = h.astype(BF16)
        o_ref[...] = jnp.zeros_like(o_ref)

    h = h_ref[...]
    a = jnp.dot(h, w1_ref[...], preferred_element_type=F32)
    b = jnp.dot(h, w3_ref[...], preferred_element_type=F32)
    u = (_silu(a) * b).astype(BF16)
    o_ref[...] += jnp.dot(u, w2_ref[...], preferred_element_type=F32)

    @pl.when(j == pl.num_programs(1) - 1)
    def _():
        gate = mod_ref[mod_base + 2:mod_base + 3, :]
        y = x_ref[...] + 0.5 * gate * o_ref[...]
        if final_norm:
            y = _rms(y, fg_ref[...])
        o_ref[...] = y


def _ffn(x, mod, g, w1, w3, w2, fg, *, seq, mod_base, final_norm):
    t, d = x.shape
    f = w1.shape[1]
    tm = min(FFN_TM, seq)
    tf = FFN_TF
    per_seq = seq // tm
    return pl.pallas_call(
        functools.partial(_ffn_body, mod_base=mod_base, final_norm=final_norm),
        grid=(t // tm, f // tf),
        in_specs=[pl.BlockSpec((tm, d), lambda i, j: (i, 0)),
                  pl.BlockSpec((None, N_MOD, d), lambda i, j: (i // per_seq, 0, 0)),
                  pl.BlockSpec((1, d), lambda i, j: (0, 0)),
                  pl.BlockSpec((d, tf), lambda i, j: (0, j)),
                  pl.BlockSpec((d, tf), lambda i, j: (0, j)),
                  pl.BlockSpec((tf, d), lambda i, j: (j, 0)),
                  pl.BlockSpec((1, d), lambda i, j: (0, 0))],
        out_specs=pl.BlockSpec((tm, d), lambda i, j: (i, 0)),
        out_shape=jax.ShapeDtypeStruct((t, d), F32),
        scratch_shapes=[pltpu.VMEM((tm, d), BF16)],
        compiler_params=_params("arbitrary", "arbitrary"),
        name="ffn_final" if final_norm else "ffn",
    )(x, mod, g, w1, w3, w2, fg)


def _rope(t, cos, sin_signed):
    return t * cos + pltpu.roll(t, LANES // 2, axis=1) * sin_signed


def _inproj_body(x_ref, mod_ref, g_ref, win_ref, cda_ref, sda_ref, cml_ref, sml_ref, m1_ref, m2_ref,
                 gq_ref, gkv_ref, wuq_ref, wukv_ref,
                 qda_ref, k1_ref, k2_ref, vtda_ref, qml_ref, kml_ref, vtml_ref):
    shift = mod_ref[3:4, :]
    scale = mod_ref[4:5, :]
    h = (_rms(x_ref[...], g_ref[...]) * (1.0 + scale) + shift).astype(BF16)
    cda, sda = cda_ref[...], sda_ref[...]
    cml, sml = cml_ref[...], sml_ref[...]
    mask1, mask2 = m1_ref[...], m2_ref[...]
    da_scale = DA_QK_DIM ** -0.5 * LOG2E
    mla_scale = (MLA_NOPE + MLA_ROPE) ** -0.5 * LOG2E

    def tile(a, k):
        return a[:, k * LANES:(k + 1) * LANES]

    pq = jnp.dot(h, win_ref[:, 0:DA_Q_W], preferred_element_type=F32)
    for hd in range(DA_HEADS):
        qda_ref[hd] = (_rope(tile(pq, hd), cda, sda) * da_scale).astype(BF16)
    pk = jnp.dot(h, win_ref[:, DA_Q_W:2 * DA_Q_W], preferred_element_type=F32)
    for hd in range(DA_HEADS):
        r = _rope(tile(pk, hd), cda, sda)
        k1_ref[hd] = (r * mask1).astype(BF16)
        k2_ref[hd] = (r * mask2).astype(BF16)
    pv = jnp.dot(h, win_ref[:, 2 * DA_Q_W:2 * DA_Q_W + DA_V_W], preferred_element_type=F32)
    for hd in range(DA_HEADS):
        vtda_ref[hd] = tile(pv, hd).T.astype(BF16)

    o3 = 2 * DA_Q_W + DA_V_W
    pr = jnp.dot(h, win_ref[:, o3:], preferred_element_type=F32)
    cq = _rms(pr[:, 0:MLA_Q_RANK], gq_ref[...]).astype(BF16)
    ckv = _rms(pr[:, MLA_Q_RANK:MLA_Q_RANK + MLA_KV_RANK], gkv_ref[...]).astype(BF16)
    kpe = _rope(pr[:, MLA_Q_RANK + MLA_KV_RANK:], cml, sml).astype(BF16)

    qm = jnp.dot(cq, wuq_ref[...], preferred_element_type=F32)
    for hd in range(MLA_HEADS):
        qml_ref[hd, :, 0:LANES] = (tile(qm, 2 * hd) * mla_scale).astype(BF16)
        qml_ref[hd, :, LANES:] = (_rope(tile(qm, 2 * hd + 1), cml, sml) * mla_scale).astype(BF16)
    kv = jnp.dot(ckv, wukv_ref[...], preferred_element_type=F32)
    for hd in range(MLA_HEADS):
        kml_ref[hd, :, 0:LANES] = tile(kv, hd).astype(BF16)
        kml_ref[hd, :, LANES:] = kpe
        vtml_ref[hd] = tile(kv, MLA_HEADS + hd).T.astype(BF16)


def _inproj(x, mod, g, win, tabs, masks, gq, gkv, wuq, wukv, *, batch, seq):
    t, d = x.shape
    tm = min(PROJ_TM, seq)
    per_seq = seq // tm
    row = lambda i: (i, 0)
    tab = lambda i: (i % per_seq, 0)
    head_rows = lambda i: (i // per_seq, 0, i % per_seq, 0)
    head_cols = lambda i: (i // per_seq, 0, 0, i % per_seq)
    hq = lambda w: pl.BlockSpec((None, DA_HEADS, tm, w), head_rows)
    vt = pl.BlockSpec((None, DA_HEADS, LANES, tm), head_cols)
    sd = jax.ShapeDtypeStruct
    return pl.pallas_call(
        _inproj_body,
        grid=(t // tm,),
        in_specs=[pl.BlockSpec((tm, d), row),
                  pl.BlockSpec((None, N_MOD, d), lambda i: (i // per_seq, 0, 0)),
                  _const_spec((1, d)),
                  _const_spec(win.shape),
                  pl.BlockSpec((tm, LANES), tab), pl.BlockSpec((tm, LANES), tab),
                  pl.BlockSpec((tm, LANES), tab), pl.BlockSpec((tm, LANES), tab),
                  _const_spec((1, LANES)), _const_spec((1, LANES)),
                  _const_spec((1, MLA_Q_RANK)), _const_spec((1, MLA_KV_RANK)),
                  _const_spec(wuq.shape), _const_spec(wukv.shape)],
        out_specs=[hq(LANES), hq(LANES), hq(LANES), vt, hq(MLA_QK_W), hq(MLA_QK_W), vt],
        out_shape=[sd((batch, DA_HEADS, seq, LANES), BF16), sd((batch, DA_HEADS, seq, LANES), BF16),
                   sd((batch, DA_HEADS, seq, LANES), BF16), sd((batch, DA_HEADS, LANES, seq), BF16),
                   sd((batch, MLA_HEADS, seq, MLA_QK_W), BF16), sd((batch, MLA_HEADS, seq, MLA_QK_W), BF16),
                   sd((batch, MLA_HEADS, LANES, seq), BF16)],
        compiler_params=_params("arbitrary"),
        name="inproj",
    )(x, mod, g, win, *tabs, *masks, gq, gkv, wuq, wukv)


class _Stream:
    def __init__(self, q_ref, k_ref, vt_ref, acc_ref, bufs, tk):
        self.q_ref, self.k_ref, self.vt_ref, self.acc_ref, self.bufs, self.tk = q_ref, k_ref, vt_ref, acc_ref, bufs, tk

    def _keys(self, blk):
        return pl.ds(pl.multiple_of(blk * self.tk, self.tk), self.tk)

    def scores(self, blk, slot):
        s = lax.dot_general(self.k_ref[self._keys(blk), :], self.q_ref[...], (((1,), (1,)), ((), ())),
                            preferred_element_type=F32)
        self.bufs[slot][...] = s
        return jnp.max(s, axis=0, keepdims=True)

    def consume(self, blk, slot, blk_max, m, l):
        m_new = jnp.maximum(m, blk_max)
        p = jnp.exp2(self.bufs[slot][...] - m_new)
        alpha = jnp.exp2(m - m_new)
        l_new = alpha * l + jnp.sum(p, axis=0, keepdims=True)
        pv = jnp.dot(self.vt_ref[:, self._keys(blk)], p.astype(BF16), preferred_element_type=F32)
        self.acc_ref[...] = alpha * self.acc_ref[...] + pv
        return m_new, l_new


def _flash(streams, nblk, tq):
    for st in streams:
        st.acc_ref[...] = jnp.zeros(st.acc_ref.shape, F32)
    m0 = jnp.full((1, tq), NEG_BIG, F32)
    l0 = jnp.zeros((1, tq), F32)
    first = tuple((m0, l0, st.scores(0, 0)) for st in streams)

    def pair(j, carry):
        b0 = 2 * j
        b2 = jnp.minimum(b0 + 2, nblk - 1)
        out = []
        for st, (m, l, max0) in zip(streams, carry):
            max1 = st.scores(b0 + 1, 1)
            m, l = st.consume(b0, 0, max0, m, l)
            max2 = st.scores(b2, 0)
            m, l = st.consume(b0 + 1, 1, max1, m, l)
            out.append((m, l, max2))
        return tuple(out)

    final = lax.fori_loop(0, nblk // 2, pair, first)
    return [l for _, l, _ in final]


def _da_body(lam_ref, g_ref, q_ref, k1_ref, k2_ref, vt_ref, o_ref, acc1_ref, acc2_ref, sa1, sb1, sa2, sb2,
             *, tk, lambda_init):
    tq = q_ref.shape[0]
    nblk = k1_ref.shape[0] // tk
    l1, l2 = _flash([_Stream(q_ref, k1_ref, vt_ref, acc1_ref, (sa1, sb1), tk),
                     _Stream(q_ref, k2_ref, vt_ref, acc2_ref, (sa2, sb2), tk)], nblk, tq)

    lq1, lk1, lq2, lk2 = (lam_ref[r:r + 1, :] for r in range(4))
    lam = (jnp.exp(jnp.sum(lq1 * lk1, axis=-1, keepdims=True))
           - jnp.exp(jnp.sum(lq2 * lk2, axis=-1, keepdims=True)) + lambda_init)
    ot = acc1_ref[...] / l1 - lam * (acc2_ref[...] / l2)
    ot = ot * lax.rsqrt(jnp.mean(ot * ot, axis=0, keepdims=True) + EPS)
    o_ref[...] = (ot.T * g_ref[...] * (1.0 - lambda_init)).astype(o_ref.dtype)


def _mla_body(q_ref, k_ref, vt_ref, o_ref, acc_ref, *bufs, tk):
    nq = acc_ref.shape[0]
    tq = q_ref.shape[0] // nq
    rows = [pl.ds(i * tq, tq) for i in range(nq)]
    streams = [_Stream(q_ref.at[rows[i], :], k_ref, vt_ref, acc_ref.at[i], bufs[2 * i:2 * i + 2], tk)
               for i in range(nq)]
    sums = _flash(streams, k_ref.shape[0] // tk, tq)
    for i in range(nq):
        o_ref[rows[i], :] = (acc_ref[i] / sums[i]).T.astype(o_ref.dtype)


def _attn_specs(batch, heads, seq, qk_w, nq=1):
    tq = min(ATT_TQ, seq // nq)
    tk = min(ATT_TK, seq)
    q_spec = pl.BlockSpec((None, None, nq * tq, qk_w), lambda b, h, i: (b, h, i, 0))
    k_spec = pl.BlockSpec((None, None, seq, qk_w), lambda b, h, i: (b, h, 0, 0))
    vt_spec = pl.BlockSpec((None, None, LANES, seq), lambda b, h, i: (b, h, 0, 0))
    o_spec = pl.BlockSpec((None, nq * tq, LANES), lambda b, h, i: (b, i, h))
    o_shape = jax.ShapeDtypeStruct((batch, seq, heads * LANES), BF16)
    return tq, tk, (batch, heads, seq // (nq * tq)), q_spec, k_spec, vt_spec, o_spec, o_shape


def _da_attn(lam_vecs, subln_g, q, k1, k2, vt, *, lambda_init):
    batch, heads, seq, _ = q.shape
    tq, tk, grid, q_spec, k_spec, vt_spec, o_spec, o_shape = _attn_specs(batch, heads, seq, LANES)
    return pl.pallas_call(
        functools.partial(_da_body, tk=tk, lambda_init=lambda_init),
        grid=grid,
        in_specs=[pl.BlockSpec(lam_vecs.shape, lambda b, h, i: (0, 0)),
                  pl.BlockSpec((1, LANES), lambda b, h, i: (0, 0)),
                  q_spec, k_spec, k_spec, vt_spec],
        out_specs=o_spec,
        out_shape=o_shape,
        scratch_shapes=[pltpu.VMEM((LANES, tq), F32)] * 2 + [pltpu.VMEM((tk, tq), F32)] * 4,
        compiler_params=_params("arbitrary", "arbitrary", "arbitrary"),
        name="da_attn",
    )(lam_vecs, subln_g, q, k1, k2, vt)


def _mla_attn(q, k, vt):
    batch, heads, seq, qk_w = q.shape
    nq = MLA_QSTREAMS
    tq, tk, grid, q_spec, k_spec, vt_spec, o_spec, o_shape = _attn_specs(batch, heads, seq, qk_w, nq)
    return pl.pallas_call(
        functools.partial(_mla_body, tk=tk),
        grid=grid,
        in_specs=[q_spec, k_spec, vt_spec],
        out_specs=o_spec,
        out_shape=o_shape,
        scratch_shapes=[pltpu.VMEM((nq, LANES, tq), F32)] + [pltpu.VMEM((tk, tq), F32)] * (2 * nq),
        compiler_params=_params("arbitrary", "arbitrary", "arbitrary"),
        name="mla_attn",
    )(q, k, vt)


def _outproj_body(x_ref, mod_ref, oda_ref, oml_ref, wa_ref, wb_ref, o_ref):
    mix = (jnp.dot(oda_ref[...], wa_ref[...], preferred_element_type=F32)
           + jnp.dot(oml_ref[...], wb_ref[...], preferred_element_type=F32))
    o_ref[...] = x_ref[...] + mod_ref[5:6, :] * mix


def _outproj(x, mod, o_da, o_mla, wo_a, wo_b, *, seq):
    t, d = x.shape
    tm = min(OUT_TM, seq)
    per_seq = seq // tm
    row = lambda i: (i, 0)
    return pl.pallas_call(
        _outproj_body,
        grid=(t // tm,),
        in_specs=[pl.BlockSpec((tm, d), row),
                  pl.BlockSpec((None, N_MOD, d), lambda i: (i // per_seq, 0, 0)),
                  pl.BlockSpec((tm, o_da.shape[1]), row),
                  pl.BlockSpec((tm, o_mla.shape[1]), row),
                  _const_spec(wo_a.shape), _const_spec(wo_b.shape)],
        out_specs=pl.BlockSpec((tm, d), row),
        out_shape=jax.ShapeDtypeStruct((t, d), F32),
        compiler_params=_params("arbitrary"),
        name="outproj",
    )(x, mod, o_da, o_mla, wo_a, wo_b)


def _da_lane_layout():
    src = np.zeros(LANES, np.int32)
    rot = np.full(LANES, -1, np.int32)
    sign = np.zeros(LANES, np.float32)
    mask = np.zeros((2, LANES), np.float32)
    half = DA_ROT // 2
    for j in range(2):
        base = j * DA_QK_DIM
        for i in range(half):
            src[j * half + i] = base + i
            rot[j * half + i] = i
            sign[j * half + i] = -1.0
            src[64 + j * half + i] = base + half + i
            rot[64 + j * half + i] = i
            sign[64 + j * half + i] = 1.0
            mask[j, j * half + i] = mask[j, 64 + j * half + i] = 1.0
        plain = DA_QK_DIM - DA_ROT
        start = DA_ROT + j * 64
        for i in range(plain):
            src[start + i] = base + DA_ROT + i
            mask[j, start + i] = 1.0
    return src, rot, sign, mask


def _mla_lane_layout():
    src = np.full(LANES, -1, np.int32)
    rot = np.full(LANES, -1, np.int32)
    sign = np.zeros(LANES, np.float32)
    half = MLA_ROPE // 2
    for i in range(half):
        src[i], rot[i], sign[i] = i, i, -1.0
        src[64 + i], rot[64 + i], sign[64 + i] = half + i, i, 1.0
    return src, rot, sign


def _take_cols(w, src):
    src = [int(v) for v in src]
    parts, start = [], 0
    for i in range(1, len(src) + 1):
        same_run = i < len(src) and ((src[i] < 0 and src[i - 1] < 0) or (src[i - 1] >= 0 and src[i] == src[i - 1] + 1))
        if not same_run:
            width = i - start
            parts.append(jnp.zeros((w.shape[0], width), w.dtype) if src[start] < 0
                         else w[:, src[start]:src[start] + width])
            start = i
    return jnp.concatenate(parts, axis=1)


def _rope_tables(seq, dim, rot, sign):
    inv = ROPE_THETA ** (-jnp.arange(0, dim, 2, dtype=F32) / dim)
    ang = jnp.arange(seq, dtype=F32)[:, None] * inv[None, :]
    cos, sin = jnp.cos(ang), jnp.sin(ang)
    on = jnp.asarray((rot >= 0).astype(np.float32))
    cos_t = _take_cols(cos, rot) * on + (1.0 - on)
    sin_t = _take_cols(sin, rot) * jnp.asarray(sign)
    return cos_t, sin_t


def _prep_layer(l, ffn1_w1, ffn1_w3, ffn1_w2, w_in, mla_w_uq, mla_w_ukv, w_o, ffn2_w1, ffn2_w3, ffn2_w2):
    da_src, _, _, _ = _da_lane_layout()
    ml_src, _, _ = _mla_lane_layout()
    head_src = np.concatenate([h * 2 * DA_QK_DIM + da_src for h in range(DA_HEADS)])
    o1, o2, o3 = DA_Q_W, 2 * DA_Q_W, 2 * DA_Q_W + DA_V_W
    o5 = o3 + MLA_Q_RANK + MLA_KV_RANK
    wi = w_in[l]
    win = jnp.concatenate([_take_cols(wi[:, :o1], head_src),
                           _take_cols(wi[:, o1:o2], head_src),
                           wi[:, o2:o5],
                           _take_cols(wi[:, o5:], ml_src)], axis=1).astype(BF16)
    per_q = MLA_NOPE + MLA_ROPE
    uq_src = np.concatenate([np.concatenate([h * per_q + np.arange(MLA_NOPE),
                                             np.where(ml_src >= 0, h * per_q + MLA_NOPE + ml_src, -1)])
                             for h in range(MLA_HEADS)])
    per_kv = MLA_NOPE + MLA_V
    ukv_src = np.concatenate([h * per_kv + np.arange(MLA_NOPE) for h in range(MLA_HEADS)]
                             + [h * per_kv + MLA_NOPE + np.arange(MLA_V) for h in range(MLA_HEADS)])
    return dict(
        f1=(ffn1_w1[l].astype(BF16), ffn1_w3[l].astype(BF16), ffn1_w2[l].astype(BF16)),
        f2=(ffn2_w1[l].astype(BF16), ffn2_w3[l].astype(BF16), ffn2_w2[l].astype(BF16)),
        win=win,
        wuq=_take_cols(mla_w_uq[l], uq_src).astype(BF16),
        wukv=_take_cols(mla_w_ukv[l], ukv_src).astype(BF16),
        wo_a=w_o[l][:DA_V_W].astype(BF16),
        wo_b=w_o[l][DA_V_W:].astype(BF16),
    )


def _encode_group(x, mods, prepped, small, final_norm_g):
    batch, seq, d = x.shape
    xt = x.reshape(batch * seq, d)
    _, da_rot, da_sign, da_mask = _da_lane_layout()
    _, ml_rot, ml_sign = _mla_lane_layout()
    tabs = _rope_tables(seq, DA_ROT, da_rot, da_sign) + _rope_tables(seq, MLA_ROPE, ml_rot, ml_sign)
    masks = (jnp.asarray(da_mask[0:1]), jnp.asarray(da_mask[1:2]))
    depth = len(prepped)
    row = lambda v: v.reshape(1, -1)
    for l in range(depth):
        p, s, mod = prepped[l], small[l], mods[l]
        lambda_init = 0.8 - 0.6 * math.exp(-0.3 * l)
        xt = _ffn(xt, mod, row(s["ffn1_norm"]), *p["f1"], row(s["ffn1_norm"]),
                  seq=seq, mod_base=0, final_norm=False)
        qda, k1, k2, vtda, qml, kml, vtml = _inproj(
            xt, mod, row(s["attn_norm"]), p["win"], tabs, masks, row(s["mla_q_norm"]), row(s["mla_kv_norm"]),
            p["wuq"], p["wukv"], batch=batch, seq=seq)
        o_da = _da_attn(s["lam_vecs"], row(s["da_subln"]), qda, k1, k2, vtda, lambda_init=lambda_init)
        o_mla = _mla_attn(qml, kml, vtml)
        xt = _outproj(xt, mod, o_da.reshape(batch * seq, -1), o_mla.reshape(batch * seq, -1),
                      p["wo_a"], p["wo_b"], seq=seq)
        last = l == depth - 1
        xt = _ffn(xt, mod, row(s["ffn2_norm"]), *p["f2"], row(final_norm_g),
                  seq=seq, mod_base=6, final_norm=last)
    return xt.reshape(batch, seq, d)


def kernel(x_prompt, x_sample, c_prompt, c_sample, ffn1_norm, ffn1_w1, ffn1_w3, ffn1_w2, attn_norm, w_in,
           da_lambda_q1, da_lambda_k1, da_lambda_q2, da_lambda_k2, da_subln, mla_q_norm, mla_w_uq, mla_kv_norm,
           mla_w_ukv, w_o, ffn2_norm, ffn2_w1, ffn2_w3, ffn2_w2, w_ada, b_ada, final_norm):
    depth = w_in.shape[0]
    nb_p, nb_s = c_prompt.shape[0], c_sample.shape[0]
    c_all = jnp.concatenate([c_prompt, c_sample], axis=0)
    rows = -(-c_all.shape[0] // 8) * 8
    c_pad = jnp.pad(c_all, ((0, rows - c_all.shape[0]), (0, 0)))

    prepped, small, mods_p, mods_s = [], [], [], []
    for l in range(depth):
        prepped.append(_prep_layer(l, ffn1_w1, ffn1_w3, ffn1_w2, w_in, mla_w_uq, mla_w_ukv, w_o,
                                   ffn2_w1, ffn2_w3, ffn2_w2))
        small.append(dict(
            ffn1_norm=ffn1_norm[l], attn_norm=attn_norm[l], ffn2_norm=ffn2_norm[l], da_subln=da_subln[l],
            mla_q_norm=mla_q_norm[l], mla_kv_norm=mla_kv_norm[l],
            lam_vecs=jnp.stack([da_lambda_q1[l], da_lambda_k1[l], da_lambda_q2[l], da_lambda_k2[l]])))
        m = _ada(c_pad, w_ada[l], b_ada[l].reshape(1, -1))
        mods_p.append(m[:nb_p].reshape(nb_p, N_MOD, D_MODEL))
        mods_s.append(m[nb_p:nb_p + nb_s].reshape(nb_s, N_MOD, D_MODEL))

    y_prompt = _encode_group(x_prompt, mods_p, prepped, small, final_norm)
    y_sample = _encode_group(x_sample, mods_s, prepped, small, final_norm)
    return (y_prompt, y_sample)
```

```python
import functools
import math

import jax
import jax.numpy as jnp
import numpy as np
from jax import lax
from jax.experimental import pallas as pl
from jax.experimental.pallas import tpu as pltpu

D_MODEL = 2048
D_FF = 5632
N_MOD = 9
DA_HEADS = 8
DA_QK_DIM = 64
DA_V_DIM = 128
DA_ROT = 16
DA_Q_W = DA_HEADS * 2 * DA_QK_DIM
DA_V_W = DA_HEADS * DA_V_DIM
MLA_HEADS = 8
MLA_NOPE = 128
MLA_ROPE = 64
MLA_V = 128
MLA_Q_RANK = 512
MLA_KV_RANK = 256
ROPE_THETA = 500000.0
EPS = 1e-6
LANES = 128
MLA_QK_W = 2 * LANES
VMEM_LIMIT = 56 * 1024 * 1024
NEG_BIG = -1e30
LOG2E = math.log2(math.e)
SHIFT_MAX_BOUND = 40.0
DA_BIAS_LANES = (8, 0)
MLA_BIAS_LANE = LANES + 32

FFN_TM, FFN_TF = 512, 512
PROJ_TM = 256
OUT_TM = 512
ATT_TQ, ATT_TK = 512, 1024
MLA_QSTREAMS = 2
ADA_TN = 1024

BF16 = jnp.bfloat16
F32 = jnp.float32


def _params(*sem):
    return pltpu.CompilerParams(dimension_semantics=sem, vmem_limit_bytes=VMEM_LIMIT)


def _rms(x, g):
    return x * lax.rsqrt(jnp.mean(x * x, axis=-1, keepdims=True) + EPS) * g


def _silu(x):
    return x / (1.0 + jnp.exp(-x))


def _const_spec(shape):
    zeros = (0,) * len(shape)
    return pl.BlockSpec(shape, lambda *_: zeros, pipeline_mode=pl.Buffered(1))


def _ada_body(c_ref, w_ref, b_ref, o_ref):
    a = _silu(c_ref[...]).astype(BF16)
    o_ref[...] = jnp.dot(a, w_ref[...].astype(BF16), preferred_element_type=F32) + b_ref[...]


def _ada(c_pad, w, b):
    rows, d = c_pad.shape
    n = w.shape[1]
    return pl.pallas_call(
        _ada_body,
        grid=(n // ADA_TN,),
        in_specs=[pl.BlockSpec((rows, d), lambda j: (0, 0)),
                  pl.BlockSpec((d, ADA_TN), lambda j: (0, j)),
                  pl.BlockSpec((1, ADA_TN), lambda j: (0, j))],
        out_specs=pl.BlockSpec((rows, ADA_TN), lambda j: (0, j)),
        out_shape=jax.ShapeDtypeStruct((rows, n), F32),
        compiler_params=_params("arbitrary"),
        name="ada",
    )(c_pad, w, b)


def _ffn_body(x_ref, mod_ref, g_ref, w1_ref, w3_ref, w2_ref, fg_ref, o_ref, h_ref, *, mod_base, final_norm):
    j = pl.program_id(1)

    @pl.when(j == 0)
    def _():
        shift = mod_ref[mod_base:mod_base + 1, :]
        scale = mod_ref[mod_base + 1:mod_base + 2, :]
        h = _rms(x_ref[...], g_ref[...]) * (1.0 + scale) + shift
        h_ref[...] = h.astype(BF16)
        o_ref[...] = jnp.zeros_like(o_ref)

    h = h_ref[...]
    a = jnp.dot(h, w1_ref[...], preferred_element_type=F32)
    b = jnp.dot(h, w3_ref[...], preferred_element_type=F32)
    u = (_silu(a) * b).astype(BF16)
    o_ref[...] += jnp.dot(u, w2_ref[...], preferred_element_type=F32)

    @pl.when(j == pl.num_programs(1) - 1)
    def _():
        gate = mod_ref[mod_base + 2:mod_base + 3, :]
        y = x_ref[...] + 0.5 * gate * o_ref[...]
        if final_norm:
            y = _rms(y, fg_ref[...])
        o_ref[...] = y


def _ffn(x, mod, g, w1, w3, w2, fg, *, seq, mod_base, final_norm):
    t, d = x.shape
    f = w1.shape[1]
    tm = min(FFN_TM, seq)
    tf = FFN_TF
    per_seq = seq // tm
    return pl.pallas_call(
        functools.partial(_ffn_body, mod_base=mod_base, final_norm=final_norm),
        grid=(t // tm, f // tf),
        in_specs=[pl.BlockSpec((tm, d), lambda i, j: (i, 0)),
                  pl.BlockSpec((None, N_MOD, d), lambda i, j: (i // per_seq, 0, 0)),
                  pl.BlockSpec((1, d), lambda i, j: (0, 0)),
                  pl.BlockSpec((d, tf), lambda i, j: (0, j)),
                  pl.BlockSpec((d, tf), lambda i, j: (0, j)),
                  pl.BlockSpec((tf, d), lambda i, j: (j, 0)),
                  pl.BlockSpec((1, d), lambda i, j: (0, 0))],
        out_specs=pl.BlockSpec((tm, d), lambda i, j: (i, 0)),
        out_shape=jax.ShapeDtypeStruct((t, d), F32),
        scratch_shapes=[pltpu.VMEM((tm, d), BF16)],
        compiler_params=_params("arbitrary", "arbitrary"),
        name="ffn_final" if final_norm else "ffn",
    )(x, mod, g, w1, w3, w2, fg)


def _rope(t, cos, sin_signed):
    return t * cos + pltpu.roll(t, LANES // 2, axis=1) * sin_signed


def _inproj_body(x_ref, mod_ref, g_ref, win_ref, cda_ref, sda_ref, cml_ref, sml_ref, m1_ref, m2_ref,
                 gq_ref, gkv_ref, wuq_ref, wukv_ref,
                 qda_ref, k1_ref, k2_ref, vtda_ref, qml_ref, kml_ref, vtml_ref):
    shift = mod_ref[3:4, :]
    scale = mod_ref[4:5, :]
    h = (_rms(x_ref[...], g_ref[...]) * (1.0 + scale) + shift).astype(BF16)
    cda, sda = cda_ref[...], sda_ref[...]
    cml, sml = cml_ref[...], sml_ref[...]
    mask1, mask2 = m1_ref[...], m2_ref[...]
    da_scale = DA_QK_DIM ** -0.5 * LOG2E
    mla_scale = (MLA_NOPE + MLA_ROPE) ** -0.5 * LOG2E

    def tile(a, k):
        return a[:, k * LANES:(k + 1) * LANES]

    def one_hot(lane):
        return (lax.broadcasted_iota(jnp.int32, (1, LANES), 1) == lane).astype(F32)

    pq = jnp.dot(h, win_ref[:, 0:DA_Q_W], preferred_element_type=F32)
    for hd in range(DA_HEADS):
        qda_ref[hd] = (_rope(tile(pq, hd), cda, sda) * da_scale).astype(BF16)
    pk = jnp.dot(h, win_ref[:, DA_Q_W:2 * DA_Q_W], preferred_element_type=F32)
    for hd in range(DA_HEADS):
        r = _rope(tile(pk, hd), cda, sda)
        k1_ref[hd] = (r * mask1 + one_hot(DA_BIAS_LANES[0])).astype(BF16)
        k2_ref[hd] = (r * mask2 + one_hot(DA_BIAS_LANES[1])).astype(BF16)
    pv = jnp.dot(h, win_ref[:, 2 * DA_Q_W:2 * DA_Q_W + DA_V_W], preferred_element_type=F32)
    for hd in range(DA_HEADS):
        vtda_ref[hd] = tile(pv, hd).T.astype(BF16)

    o3 = 2 * DA_Q_W + DA_V_W
    pr = jnp.dot(h, win_ref[:, o3:], preferred_element_type=F32)
    cq = _rms(pr[:, 0:MLA_Q_RANK], gq_ref[...]).astype(BF16)
    ckv = _rms(pr[:, MLA_Q_RANK:MLA_Q_RANK + MLA_KV_RANK], gkv_ref[...]).astype(BF16)
    kpe = (_rope(pr[:, MLA_Q_RANK + MLA_KV_RANK:], cml, sml) + one_hot(MLA_BIAS_LANE - LANES)).astype(BF16)

    qm = jnp.dot(cq, wuq_ref[...], preferred_element_type=F32)
    for hd in range(MLA_HEADS):
        qml_ref[hd, :, 0:LANES] = (tile(qm, 2 * hd) * mla_scale).astype(BF16)
        qml_ref[hd, :, LANES:] = (_rope(tile(qm, 2 * hd + 1), cml, sml) * mla_scale).astype(BF16)
    kv = jnp.dot(ckv, wukv_ref[...], preferred_element_type=F32)
    for hd in range(MLA_HEADS):
        kml_ref[hd, :, 0:LANES] = tile(kv, hd).astype(BF16)
        kml_ref[hd, :, LANES:] = kpe
        vtml_ref[hd] = tile(kv, MLA_HEADS + hd).T.astype(BF16)


def _inproj(x, mod, g, win, tabs, masks, gq, gkv, wuq, wukv, *, batch, seq):
    t, d = x.shape
    tm = min(PROJ_TM, seq)
    per_seq = seq // tm
    row = lambda i: (i, 0)
    tab = lambda i: (i % per_seq, 0)
    head_rows = lambda i: (i // per_seq, 0, i % per_seq, 0)
    head_cols = lambda i: (i // per_seq, 0, 0, i % per_seq)
    hq = lambda w: pl.BlockSpec((None, DA_HEADS, tm, w), head_rows)
    vt = pl.BlockSpec((None, DA_HEADS, LANES, tm), head_cols)
    sd = jax.ShapeDtypeStruct
    return pl.pallas_call(
        _inproj_body,
        grid=(t // tm,),
        in_specs=[pl.BlockSpec((tm, d), row),
                  pl.BlockSpec((None, N_MOD, d), lambda i: (i // per_seq, 0, 0)),
                  _const_spec((1, d)),
                  _const_spec(win.shape),
                  pl.BlockSpec((tm, LANES), tab), pl.BlockSpec((tm, LANES), tab),
                  pl.BlockSpec((tm, LANES), tab), pl.BlockSpec((tm, LANES), tab),
                  _const_spec((1, LANES)), _const_spec((1, LANES)),
                  _const_spec((1, MLA_Q_RANK)), _const_spec((1, MLA_KV_RANK)),
                  _const_spec(wuq.shape), _const_spec(wukv.shape)],
        out_specs=[hq(LANES), hq(LANES), hq(LANES), vt, hq(MLA_QK_W), hq(MLA_QK_W), vt],
        out_shape=[sd((batch, DA_HEADS, seq, LANES), BF16), sd((batch, DA_HEADS, seq, LANES), BF16),
                   sd((batch, DA_HEADS, seq, LANES), BF16), sd((batch, DA_HEADS, LANES, seq), BF16),
                   sd((batch, MLA_HEADS, seq, MLA_QK_W), BF16), sd((batch, MLA_HEADS, seq, MLA_QK_W), BF16),
                   sd((batch, MLA_HEADS, LANES, seq), BF16)],
        compiler_params=_params("arbitrary"),
        name="inproj",
    )(x, mod, g, win, *tabs, *masks, gq, gkv, wuq, wukv)


class _Stream:
    def __init__(self, q_ref, q_mask, bias_lane, k_ref, vt_ref, acc_ref, sum_ref, qz_ref, k2max_ref, bufs, tk):
        self.q_ref, self.q_mask, self.bias_lane, self.k_ref, self.vt_ref = q_ref, q_mask, bias_lane, k_ref, vt_ref
        self.acc_ref, self.sum_ref, self.qz_ref, self.k2max_ref, self.bufs, self.tk = (
            acc_ref, sum_ref, qz_ref, k2max_ref, bufs, tk)

    def _keys(self, blk):
        return pl.ds(pl.multiple_of(blk * self.tk, self.tk), self.tk)

    def masked_q(self):
        q = self.q_ref[...].astype(F32)
        return q if self.q_mask is None else q * self.q_mask

    def store_key_norm(self, nblk):
        def blk(b, best):
            k = self.k_ref[self._keys(b), :].astype(F32)
            return jnp.maximum(best, jnp.max(jnp.sum(k * k, axis=1, keepdims=True), axis=0, keepdims=True))
        best = lax.fori_loop(0, nblk, blk, jnp.zeros((1, 1), F32))
        self.k2max_ref[...] = jnp.broadcast_to(best, self.k2max_ref.shape)

    def score_bound(self):
        q = self.masked_q()
        q2 = jnp.max(jnp.sum(q * q, axis=1, keepdims=True), axis=0, keepdims=True)
        return jnp.sqrt(q2 * self.k2max_ref[0:1, 0:1]) * 1.01 + 0.01

    def set_query(self, shift):
        q = self.masked_q()
        if shift is not None:
            lane = lax.broadcasted_iota(jnp.int32, q.shape, 1)
            q = jnp.where(lane == self.bias_lane, -shift, q)
        self.qz_ref[...] = q.astype(BF16)

    def _qk(self, blk):
        return lax.dot_general(self.k_ref[self._keys(blk), :], self.qz_ref[...], (((1,), (1,)), ((), ())),
                               preferred_element_type=F32)

    def _pv(self, blk, p):
        return jnp.dot(self.vt_ref[:, self._keys(blk)], p.astype(BF16), preferred_element_type=F32)

    def scores(self, blk, slot):
        s = self._qk(blk)
        self.bufs[slot][...] = s
        return jnp.max(s, axis=0, keepdims=True)

    def consume(self, blk, slot, blk_max, m, l):
        m_new = jnp.maximum(m, blk_max)
        p = jnp.exp2(self.bufs[slot][...] - m_new)
        alpha = jnp.exp2(m - m_new)
        l_new = alpha * l + jnp.sum(p, axis=0, keepdims=True)
        self.acc_ref[...] = alpha * self.acc_ref[...] + self._pv(blk, p)
        return m_new, l_new

    def accumulate(self, blk, part):
        p = jnp.exp2(self._qk(blk))
        self.acc_ref[...] += self._pv(blk, p)
        return part + jnp.sum(p.reshape(self.tk // 8, 8, p.shape[1]), axis=0)


def _flash_exact(streams, nblk, tq):
    m0 = jnp.full((1, tq), NEG_BIG, F32)
    l0 = jnp.zeros((1, tq), F32)
    for st in streams:
        st.set_query(None)
    first = tuple((m0, l0, st.scores(0, 0)) for st in streams)

    def pair(j, carry):
        b0 = 2 * j
        b2 = jnp.minimum(b0 + 2, nblk - 1)
        out = []
        for st, (m, l, max0) in zip(streams, carry):
            max1 = st.scores(b0 + 1, 1)
            m, l = st.consume(b0, 0, max0, m, l)
            max2 = st.scores(b2, 0)
            m, l = st.consume(b0 + 1, 1, max1, m, l)
            out.append((m, l, max2))
        return tuple(out)

    final = lax.fori_loop(0, nblk // 2, pair, first)
    for st, (_, l, _) in zip(streams, final):
        st.sum_ref[...] = l


def _flash_shifted(streams, bounds, nblk, tq):
    for st, bound in zip(streams, bounds):
        st.set_query(bound)

    def pair(j, parts):
        return tuple(st.accumulate(2 * j + 1, st.accumulate(2 * j, part)) for st, part in zip(streams, parts))

    parts = lax.fori_loop(0, nblk // 2, pair, tuple(jnp.zeros((8, tq), F32) for _ in streams))
    for st, part in zip(streams, parts):
        st.sum_ref[...] = jnp.sum(part, axis=0, keepdims=True)


def _flash(streams, nblk, tq):
    assert nblk % 2 == 0

    @pl.when(pl.program_id(2) == 0)
    def _():
        for st in {id(st.k2max_ref): st for st in streams}.values():
            st.store_key_norm(nblk)

    for st in streams:
        st.acc_ref[...] = jnp.zeros(st.acc_ref.shape, F32)
    bounds = [st.score_bound() for st in streams]
    shift_ok = functools.reduce(jnp.maximum, bounds)[0, 0] <= SHIFT_MAX_BOUND

    @pl.when(shift_ok)
    def _():
        _flash_shifted(streams, bounds, nblk, tq)

    @pl.when(jnp.logical_not(shift_ok))
    def _():
        _flash_exact(streams, nblk, tq)


def _da_body(lam_ref, g_ref, m1_ref, m2_ref, q_ref, k1_ref, k2_ref, vt_ref, o_ref,
             acc_ref, sum_ref, qz_ref, k2max_ref, *bufs, tk, lambda_init):
    tq = q_ref.shape[0]
    masks = (m1_ref[...], m2_ref[...])
    streams = [_Stream(q_ref, masks[j], DA_BIAS_LANES[j], (k1_ref, k2_ref)[j], vt_ref, acc_ref.at[j], sum_ref.at[j],
                       qz_ref.at[j], k2max_ref.at[j], bufs[2 * j:2 * j + 2], tk) for j in range(2)]
    _flash(streams, k1_ref.shape[0] // tk, tq)

    lq1, lk1, lq2, lk2 = (lam_ref[r:r + 1, :] for r in range(4))
    lam = (jnp.exp(jnp.sum(lq1 * lk1, axis=-1, keepdims=True))
           - jnp.exp(jnp.sum(lq2 * lk2, axis=-1, keepdims=True)) + lambda_init)
    ot = acc_ref[0] / sum_ref[0] - lam * (acc_ref[1] / sum_ref[1])
    ot = ot * lax.rsqrt(jnp.mean(ot * ot, axis=0, keepdims=True) + EPS)
    o_ref[...] = (ot.T * g_ref[...] * (1.0 - lambda_init)).astype(o_ref.dtype)


def _mla_body(q_ref, k_ref, vt_ref, o_ref, acc_ref, sum_ref, qz_ref, k2max_ref, *bufs, tk):
    nq = acc_ref.shape[0]
    tq = q_ref.shape[0] // nq
    rows = [pl.ds(i * tq, tq) for i in range(nq)]
    key_norm = k2max_ref.at[0]
    streams = [_Stream(q_ref.at[rows[i], :], None, MLA_BIAS_LANE, k_ref, vt_ref, acc_ref.at[i], sum_ref.at[i],
                       qz_ref.at[i], key_norm, bufs[2 * i:2 * i + 2], tk) for i in range(nq)]
    _flash(streams, k_ref.shape[0] // tk, tq)
    for i in range(nq):
        o_ref[rows[i], :] = (acc_ref[i] / sum_ref[i]).T.astype(o_ref.dtype)


def _attn_specs(batch, heads, seq, qk_w, nq=1):
    tq = min(ATT_TQ, seq // nq)
    tk = min(ATT_TK, seq)
    q_spec = pl.BlockSpec((None, None, nq * tq, qk_w), lambda b, h, i: (b, h, i, 0))
    k_spec = pl.BlockSpec((None, None, seq, qk_w), lambda b, h, i: (b, h, 0, 0))
    vt_spec = pl.BlockSpec((None, None, LANES, seq), lambda b, h, i: (b, h, 0, 0))
    o_spec = pl.BlockSpec((None, nq * tq, LANES), lambda b, h, i: (b, i, h))
    o_shape = jax.ShapeDtypeStruct((batch, seq, heads * LANES), BF16)
    return tq, tk, (batch, heads, seq // (nq * tq)), q_spec, k_spec, vt_spec, o_spec, o_shape


def _attn_scratch(streams, key_sets, tq, tk, qk_w):
    return ([pltpu.VMEM((streams, LANES, tq), F32),
             pltpu.VMEM((streams, 1, tq), F32),
             pltpu.VMEM((streams, tq, qk_w), BF16),
             pltpu.VMEM((key_sets, 8, LANES), F32)]
            + [pltpu.VMEM((tk, tq), F32)] * (2 * streams))


def _da_attn(lam_vecs, subln_g, masks, q, k1, k2, vt, *, lambda_init):
    batch, heads, seq, _ = q.shape
    tq, tk, grid, q_spec, k_spec, vt_spec, o_spec, o_shape = _attn_specs(batch, heads, seq, LANES)
    lane_row = pl.BlockSpec((1, LANES), lambda b, h, i: (0, 0))
    return pl.pallas_call(
        functools.partial(_da_body, tk=tk, lambda_init=lambda_init),
        grid=grid,
        in_specs=[pl.BlockSpec(lam_vecs.shape, lambda b, h, i: (0, 0)), lane_row, lane_row, lane_row,
                  q_spec, k_spec, k_spec, vt_spec],
        out_specs=o_spec,
        out_shape=o_shape,
        scratch_shapes=_attn_scratch(2, 2, tq, tk, LANES),
        compiler_params=_params("arbitrary", "arbitrary", "arbitrary"),
        name="da_attn",
    )(lam_vecs, subln_g, *masks, q, k1, k2, vt)


def _mla_attn(q, k, vt):
    batch, heads, seq, qk_w = q.shape
    nq = MLA_QSTREAMS
    tq, tk, grid, q_spec, k_spec, vt_spec, o_spec, o_shape = _attn_specs(batch, heads, seq, qk_w, nq)
    return pl.pallas_call(
        functools.partial(_mla_body, tk=tk),
        grid=grid,
        in_specs=[q_spec, k_spec, vt_spec],
        out_specs=o_spec,
        out_shape=o_shape,
        scratch_shapes=_attn_scratch(nq, 1, tq, tk, qk_w),
        compiler_params=_params("arbitrary", "arbitrary", "arbitrary"),
        name="mla_attn",
    )(q, k, vt)


def _outproj_body(x_ref, mod_ref, oda_ref, oml_ref, wa_ref, wb_ref, o_ref):
    mix = (jnp.dot(oda_ref[...], wa_ref[...], preferred_element_type=F32)
           + jnp.dot(oml_ref[...], wb_ref[...], preferred_element_type=F32))
    o_ref[...] = x_ref[...] + mod_ref[5:6, :] * mix


def _outproj(x, mod, o_da, o_mla, wo_a, wo_b, *, seq):
    t, d = x.shape
    tm = min(OUT_TM, seq)
    per_seq = seq // tm
    row = lambda i: (i, 0)
    return pl.pallas_call(
        _outproj_body,
        grid=(t // tm,),
        in_specs=[pl.BlockSpec((tm, d), row),
                  pl.BlockSpec((None, N_MOD, d), lambda i: (i // per_seq, 0, 0)),
                  pl.BlockSpec((tm, o_da.shape[1]), row),
                  pl.BlockSpec((tm, o_mla.shape[1]), row),
                  _const_spec(wo_a.shape), _const_spec(wo_b.shape)],
        out_specs=pl.BlockSpec((tm, d), row),
        out_shape=jax.ShapeDtypeStruct((t, d), F32),
        compiler_params=_params("arbitrary"),
        name="outproj",
    )(x, mod, o_da, o_mla, wo_a, wo_b)


def _da_lane_layout():
    src = np.zeros(LANES, np.int32)
    rot = np.full(LANES, -1, np.int32)
    sign = np.zeros(LANES, np.float32)
    mask = np.zeros((2, LANES), np.float32)
    half = DA_ROT // 2
    for j in range(2):
        base = j * DA_QK_DIM
        for i in range(half):
            src[j * half + i] = base + i
            rot[j * half + i] = i
            sign[j * half + i] = -1.0
            src[64 + j * half + i] = base + half + i
            rot[64 + j * half + i] = i
            sign[64 + j * half + i] = 1.0
            mask[j, j * half + i] = mask[j, 64 + j * half + i] = 1.0
        plain = DA_QK_DIM - DA_ROT
        start = DA_ROT + j * 64
        for i in range(plain):
            src[start + i] = base + DA_ROT + i
            mask[j, start + i] = 1.0
    return src, rot, sign, mask


def _mla_lane_layout():
    src = np.full(LANES, -1, np.int32)
    rot = np.full(LANES, -1, np.int32)
    sign = np.zeros(LANES, np.float32)
    half = MLA_ROPE // 2
    for i in range(half):
        src[i], rot[i], sign[i] = i, i, -1.0
        src[64 + i], rot[64 + i], sign[64 + i] = half + i, i, 1.0
    return src, rot, sign


def _take_cols(w, src):
    src = [int(v) for v in src]
    parts, start = [], 0
    for i in range(1, len(src) + 1):
        same_run = i < len(src) and ((src[i] < 0 and src[i - 1] < 0) or (src[i - 1] >= 0 and src[i] == src[i - 1] + 1))
        if not same_run:
            width = i - start
            parts.append(jnp.zeros((w.shape[0], width), w.dtype) if src[start] < 0
                         else w[:, src[start]:src[start] + width])
            start = i
    return jnp.concatenate(parts, axis=1)


def _rope_tables(seq, dim, rot, sign):
    inv = ROPE_THETA ** (-jnp.arange(0, dim, 2, dtype=F32) / dim)
    ang = jnp.arange(seq, dtype=F32)[:, None] * inv[None, :]
    cos, sin = jnp.cos(ang), jnp.sin(ang)
    on = jnp.asarray((rot >= 0).astype(np.float32))
    cos_t = _take_cols(cos, rot) * on + (1.0 - on)
    sin_t = _take_cols(sin, rot) * jnp.asarray(sign)
    return cos_t, sin_t


def _prep_layer(l, ffn1_w1, ffn1_w3, ffn1_w2, w_in, mla_w_uq, mla_w_ukv, w_o, ffn2_w1, ffn2_w3, ffn2_w2):
    da_src, _, _, _ = _da_lane_layout()
    ml_src, _, _ = _mla_lane_layout()
    head_src = np.concatenate([h * 2 * DA_QK_DIM + da_src for h in range(DA_HEADS)])
    o1, o2, o3 = DA_Q_W, 2 * DA_Q_W, 2 * DA_Q_W + DA_V_W
    o5 = o3 + MLA_Q_RANK + MLA_KV_RANK
    wi = w_in[l]
    win = jnp.concatenate([_take_cols(wi[:, :o1], head_src),
                           _take_cols(wi[:, o1:o2], head_src),
                           wi[:, o2:o5],
                           _take_cols(wi[:, o5:], ml_src)], axis=1).astype(BF16)
    per_q = MLA_NOPE + MLA_ROPE
    uq_src = np.concatenate([np.concatenate([h * per_q + np.arange(MLA_NOPE),
                                             np.where(ml_src >= 0, h * per_q + MLA_NOPE + ml_src, -1)])
                             for h in range(MLA_HEADS)])
    per_kv = MLA_NOPE + MLA_V
    ukv_src = np.concatenate([h * per_kv + np.arange(MLA_NOPE) for h in range(MLA_HEADS)]
                             + [h * per_kv + MLA_NOPE + np.arange(MLA_V) for h in range(MLA_HEADS)])
    return dict(
        f1=(ffn1_w1[l].astype(BF16), ffn1_w3[l].astype(BF16), ffn1_w2[l].astype(BF16)),
        f2=(ffn2_w1[l].astype(BF16), ffn2_w3[l].astype(BF16), ffn2_w2[l].astype(BF16)),
        win=win,
        wuq=_take_cols(mla_w_uq[l], uq_src).astype(BF16),
        wukv=_take_cols(mla_w_ukv[l], ukv_src).astype(BF16),
        wo_a=w_o[l][:DA_V_W].astype(BF16),
        wo_b=w_o[l][DA_V_W:].astype(BF16),
    )


def _encode_group(x, mods, prepped, small, final_norm_g):
    batch, seq, d = x.shape
    xt = x.reshape(batch * seq, d)
    _, da_rot, da_sign, da_mask = _da_lane_layout()
    _, ml_rot, ml_sign = _mla_lane_layout()
    tabs = _rope_tables(seq, DA_ROT, da_rot, da_sign) + _rope_tables(seq, MLA_ROPE, ml_rot, ml_sign)
    masks = (jnp.asarray(da_mask[0:1]), jnp.asarray(da_mask[1:2]))
    depth = len(prepped)
    row = lambda v: v.reshape(1, -1)
    for l in range(depth):
        p, s, mod = prepped[l], small[l], mods[l]
        lambda_init = 0.8 - 0.6 * math.exp(-0.3 * l)
        xt = _ffn(xt, mod, row(s["ffn1_norm"]), *p["f1"], row(s["ffn1_norm"]),
                  seq=seq, mod_base=0, final_norm=False)
        qda, k1, k2, vtda, qml, kml, vtml = _inproj(
            xt, mod, row(s["attn_norm"]), p["win"], tabs, masks, row(s["mla_q_norm"]), row(s["mla_kv_norm"]),
            p["wuq"], p["wukv"], batch=batch, seq=seq)
        o_da = _da_attn(s["lam_vecs"], row(s["da_subln"]), masks, qda, k1, k2, vtda, lambda_init=lambda_init)
        o_mla = _mla_attn(qml, kml, vtml)
        xt = _outproj(xt, mod, o_da.reshape(batch * seq, -1), o_mla.reshape(batch * seq, -1),
                      p["wo_a"], p["wo_b"], seq=seq)
        last = l == depth - 1
        xt = _ffn(xt, mod, row(s["ffn2_norm"]), *p["f2"], row(final_norm_g),
                  seq=seq, mod_base=6, final_norm=last)
    return xt.reshape(batch, seq, d)


def kernel(x_prompt, x_sample, c_prompt, c_sample, ffn1_norm, ffn1_w1, ffn1_w3, ffn1_w2, attn_norm, w_in,
           da_lambda_q1, da_lambda_k1, da_lambda_q2, da_lambda_k2, da_subln, mla_q_norm, mla_w_uq, mla_kv_norm,
           mla_w_ukv, w_o, ffn2_norm, ffn2_w1, ffn2_w3, ffn2_w2, w_ada, b_ada, final_norm):
    depth = w_in.shape[0]
    nb_p, nb_s = c_prompt.shape[0], c_sample.shape[0]
    c_all = jnp.concatenate([c_prompt, c_sample], axis=0)
    rows = -(-c_all.shape[0] // 8) * 8
    c_pad = jnp.pad(c_all, ((0, rows - c_all.shape[0]), (0, 0)))

    prepped, small, mods_p, mods_s = [], [], [], []
    for l in range(depth):
        prepped.append(_prep_layer(l, ffn1_w1, ffn1_w3, ffn1_w2, w_in, mla_w_uq, mla_w_ukv, w_o,
                                   ffn2_w1, ffn2_w3, ffn2_w2))
        small.append(dict(
            ffn1_norm=ffn1_norm[l], attn_norm=attn_norm[l], ffn2_norm=ffn2_norm[l], da_subln=da_subln[l],
            mla_q_norm=mla_q_norm[l], mla_kv_norm=mla_kv_norm[l],
            lam_vecs=jnp.stack([da_lambda_q1[l], da_lambda_k1[l], da_lambda_q2[l], da_lambda_k2[l]])))
        m = _ada(c_pad, w_ada[l], b_ada[l].reshape(1, -1))
        mods_p.append(m[:nb_p].reshape(nb_p, N_MOD, D_MODEL))
        mods_s.append(m[nb_p:nb_p + nb_s].reshape(nb_s, N_MOD, D_MODEL))

    y_prompt = _encode_group(x_prompt, mods_p, prepped, small, final_norm)
    y_sample = _encode_group(x_sample, mods_s, prepped, small, final_norm)
    return (y_prompt, y_sample)
```

```python
import functools
import math

import jax
import jax.numpy as jnp
import numpy as np
from jax import lax
from jax.experimental import pallas as pl
from jax.experimental.pallas import tpu as pltpu

D_MODEL = 2048
D_FF = 5632
N_MOD = 9
DA_HEADS = 8
DA_QK_DIM = 64
DA_V_DIM = 128
DA_ROT = 16
DA_Q_W = DA_HEADS * 2 * DA_QK_DIM
DA_V_W = DA_HEADS * DA_V_DIM
MLA_HEADS = 8
MLA_NOPE = 128
MLA_ROPE = 64
MLA_V = 128
MLA_Q_RANK = 512
MLA_KV_RANK = 256
ROPE_THETA = 500000.0
EPS = 1e-6
LANES = 128
MLA_QK_W = 2 * LANES
VMEM_LIMIT = 56 * 1024 * 1024
FFN_VMEM_LIMIT = 60 * 1024 * 1024
NEG_BIG = -1e30
LOG2E = math.log2(math.e)
SHIFT_MAX_BOUND = 40.0
DA_BIAS_LANES = (8, 0)
MLA_BIAS_LANE = LANES + 32

FFN_TM, FFN_TF = 1024, 512
FFN_ROW_CHUNK = 128
PROJ_TM = 256
OUT_TM = 512
ATT_TQ, ATT_TK = 512, 2048
MLA_QSTREAMS = 2
ADA_TN = 1024

BF16 = jnp.bfloat16
F32 = jnp.float32


def _params(*sem, vmem=VMEM_LIMIT):
    return pltpu.CompilerParams(dimension_semantics=sem, vmem_limit_bytes=vmem)


def _rms(x, g):
    return x * lax.rsqrt(jnp.mean(x * x, axis=-1, keepdims=True) + EPS) * g


def _silu(x):
    return x / (1.0 + jnp.exp(-x))


def _const_spec(shape):
    zeros = (0,) * len(shape)
    return pl.BlockSpec(shape, lambda *_: zeros, pipeline_mode=pl.Buffered(1))


def _ada_body(c_ref, w_ref, b_ref, o_ref):
    a = _silu(c_ref[...]).astype(BF16)
    o_ref[...] = jnp.dot(a, w_ref[...].astype(BF16), preferred_element_type=F32) + b_ref[...]


def _ada(c_pad, w, b):
    rows, d = c_pad.shape
    n = w.shape[1]
    return pl.pallas_call(
        _ada_body,
        grid=(n // ADA_TN,),
        in_specs=[pl.BlockSpec((rows, d), lambda j: (0, 0)),
                  pl.BlockSpec((d, ADA_TN), lambda j: (0, j)),
                  pl.BlockSpec((1, ADA_TN), lambda j: (0, j))],
        out_specs=pl.BlockSpec((rows, ADA_TN), lambda j: (0, j)),
        out_shape=jax.ShapeDtypeStruct((rows, n), F32),
        compiler_params=_params("arbitrary"),
        name="ada",
    )(c_pad, w, b)


def _ffn_body(x_ref, mod_ref, g_ref, w1_ref, w3_ref, w2_ref, fg_ref, o_ref, h_ref, *, mod_base, final_norm):
    j = pl.program_id(1)

    def row_chunks(fn):
        def body(c, carry):
            fn(pl.ds(pl.multiple_of(c * FFN_ROW_CHUNK, FFN_ROW_CHUNK), FFN_ROW_CHUNK))
            return carry
        lax.fori_loop(0, x_ref.shape[0] // FFN_ROW_CHUNK, body, 0)

    @pl.when(j == 0)
    def _():
        shift = mod_ref[mod_base:mod_base + 1, :]
        scale = mod_ref[mod_base + 1:mod_base + 2, :]

        def prenorm(rows):
            h = _rms(x_ref[rows, :], g_ref[...]) * (1.0 + scale) + shift
            h_ref[rows, :] = h.astype(BF16)
            o_ref[rows, :] = jnp.zeros((FFN_ROW_CHUNK, o_ref.shape[1]), F32)
        row_chunks(prenorm)

    h = h_ref[...]
    a = jnp.dot(h, w1_ref[...], preferred_element_type=F32)
    b = jnp.dot(h, w3_ref[...], preferred_element_type=F32)
    u = (_silu(a) * b).astype(BF16)
    o_ref[...] += jnp.dot(u, w2_ref[...], preferred_element_type=F32)

    @pl.when(j == pl.num_programs(1) - 1)
    def _():
        gate = mod_ref[mod_base + 2:mod_base + 3, :]

        def residual(rows):
            y = x_ref[rows, :] + 0.5 * gate * o_ref[rows, :]
            if final_norm:
                y = _rms(y, fg_ref[...])
            o_ref[rows, :] = y
        row_chunks(residual)


def _ffn(x, mod, g, w1, w3, w2, fg, *, seq, mod_base, final_norm):
    t, d = x.shape
    f = w1.shape[1]
    tm = min(FFN_TM, seq)
    tf = FFN_TF
    per_seq = seq // tm
    return pl.pallas_call(
        functools.partial(_ffn_body, mod_base=mod_base, final_norm=final_norm),
        grid=(t // tm, f // tf),
        in_specs=[pl.BlockSpec((tm, d), lambda i, j: (i, 0)),
                  pl.BlockSpec((None, N_MOD, d), lambda i, j: (i // per_seq, 0, 0)),
                  pl.BlockSpec((1, d), lambda i, j: (0, 0)),
                  pl.BlockSpec((d, tf), lambda i, j: (0, j)),
                  pl.BlockSpec((d, tf), lambda i, j: (0, j)),
                  pl.BlockSpec((tf, d), lambda i, j: (j, 0)),
                  pl.BlockSpec((1, d), lambda i, j: (0, 0))],
        out_specs=pl.BlockSpec((tm, d), lambda i, j: (i, 0)),
        out_shape=jax.ShapeDtypeStruct((t, d), F32),
        scratch_shapes=[pltpu.VMEM((tm, d), BF16)],
        compiler_params=_params("arbitrary", "arbitrary", vmem=FFN_VMEM_LIMIT),
        name="ffn_final" if final_norm else "ffn",
    )(x, mod, g, w1, w3, w2, fg)


def _rope(t, cos, sin_signed):
    return t * cos + pltpu.roll(t, LANES // 2, axis=1) * sin_signed


def _inproj_body(x_ref, mod_ref, g_ref, win_ref, cda_ref, sda_ref, cml_ref, sml_ref, m1_ref, m2_ref,
                 gq_ref, gkv_ref, wuq_ref, wukv_ref,
                 qda_ref, k1_ref, k2_ref, vtda_ref, qml_ref, kml_ref, vtml_ref):
    shift = mod_ref[3:4, :]
    scale = mod_ref[4:5, :]
    h = (_rms(x_ref[...], g_ref[...]) * (1.0 + scale) + shift).astype(BF16)
    cda, sda = cda_ref[...], sda_ref[...]
    cml, sml = cml_ref[...], sml_ref[...]
    mask1, mask2 = m1_ref[...], m2_ref[...]
    da_scale = DA_QK_DIM ** -0.5 * LOG2E
    mla_scale = (MLA_NOPE + MLA_ROPE) ** -0.5 * LOG2E

    def tile(a, k):
        return a[:, k * LANES:(k + 1) * LANES]

    def one_hot(lane):
        return (lax.broadcasted_iota(jnp.int32, (1, LANES), 1) == lane).astype(F32)

    pq = jnp.dot(h, win_ref[:, 0:DA_Q_W], preferred_element_type=F32)
    for hd in range(DA_HEADS):
        qda_ref[hd] = (_rope(tile(pq, hd), cda, sda) * da_scale).astype(BF16)
    pk = jnp.dot(h, win_ref[:, DA_Q_W:2 * DA_Q_W], preferred_element_type=F32)
    for hd in range(DA_HEADS):
        r = _rope(tile(pk, hd), cda, sda)
        k1_ref[hd] = (r * mask1 + one_hot(DA_BIAS_LANES[0])).astype(BF16)
        k2_ref[hd] = (r * mask2 + one_hot(DA_BIAS_LANES[1])).astype(BF16)
    pv = jnp.dot(h, win_ref[:, 2 * DA_Q_W:2 * DA_Q_W + DA_V_W], preferred_element_type=F32)
    for hd in range(DA_HEADS):
        vtda_ref[hd] = tile(pv, hd).T.astype(BF16)

    o3 = 2 * DA_Q_W + DA_V_W
    pr = jnp.dot(h, win_ref[:, o3:], preferred_element_type=F32)
    cq = _rms(pr[:, 0:MLA_Q_RANK], gq_ref[...]).astype(BF16)
    ckv = _rms(pr[:, MLA_Q_RANK:MLA_Q_RANK + MLA_KV_RANK], gkv_ref[...]).astype(BF16)
    kpe = (_rope(pr[:, MLA_Q_RANK + MLA_KV_RANK:], cml, sml) + one_hot(MLA_BIAS_LANE - LANES)).astype(BF16)

    qm = jnp.dot(cq, wuq_ref[...], preferred_element_type=F32)
    for hd in range(MLA_HEADS):
        qml_ref[hd, :, 0:LANES] = (tile(qm, 2 * hd) * mla_scale).astype(BF16)
        qml_ref[hd, :, LANES:] = (_rope(tile(qm, 2 * hd + 1), cml, sml) * mla_scale).astype(BF16)
    kv = jnp.dot(ckv, wukv_ref[...], preferred_element_type=F32)
    for hd in range(MLA_HEADS):
        kml_ref[hd, :, 0:LANES] = tile(kv, hd).astype(BF16)
        kml_ref[hd, :, LANES:] = kpe
        vtml_ref[hd] = tile(kv, MLA_HEADS + hd).T.astype(BF16)


def _inproj(x, mod, g, win, tabs, masks, gq, gkv, wuq, wukv, *, batch, seq):
    t, d = x.shape
    tm = min(PROJ_TM, seq)
    per_seq = seq // tm
    row = lambda i: (i, 0)
    tab = lambda i: (i % per_seq, 0)
    head_rows = lambda i: (i // per_seq, 0, i % per_seq, 0)
    head_cols = lambda i: (i // per_seq, 0, 0, i % per_seq)
    hq = lambda w: pl.BlockSpec((None, DA_HEADS, tm, w), head_rows)
    vt = pl.BlockSpec((None, DA_HEADS, LANES, tm), head_cols)
    sd = jax.ShapeDtypeStruct
    return pl.pallas_call(
        _inproj_body,
        grid=(t // tm,),
        in_specs=[pl.BlockSpec((tm, d), row),
                  pl.BlockSpec((None, N_MOD, d), lambda i: (i // per_seq, 0, 0)),
                  _const_spec((1, d)),
                  _const_spec(win.shape),
                  pl.BlockSpec((tm, LANES), tab), pl.BlockSpec((tm, LANES), tab),
                  pl.BlockSpec((tm, LANES), tab), pl.BlockSpec((tm, LANES), tab),
                  _const_spec((1, LANES)), _const_spec((1, LANES)),
                  _const_spec((1, MLA_Q_RANK)), _const_spec((1, MLA_KV_RANK)),
                  _const_spec(wuq.shape), _const_spec(wukv.shape)],
        out_specs=[hq(LANES), hq(LANES), hq(LANES), vt, hq(MLA_QK_W), hq(MLA_QK_W), vt],
        out_shape=[sd((batch, DA_HEADS, seq, LANES), BF16), sd((batch, DA_HEADS, seq, LANES), BF16),
                   sd((batch, DA_HEADS, seq, LANES), BF16), sd((batch, DA_HEADS, LANES, seq), BF16),
                   sd((batch, MLA_HEADS, seq, MLA_QK_W), BF16), sd((batch, MLA_HEADS, seq, MLA_QK_W), BF16),
                   sd((batch, MLA_HEADS, LANES, seq), BF16)],
        compiler_params=_params("arbitrary"),
        name="inproj",
    )(x, mod, g, win, *tabs, *masks, gq, gkv, wuq, wukv)


class _Stream:
    def __init__(self, q_ref, q_mask, bias_lane, k_ref, vt_ref, acc_ref, sum_ref, qz_ref, k2max_ref, bufs, tk):
        self.q_ref, self.q_mask, self.bias_lane, self.k_ref, self.vt_ref = q_ref, q_mask, bias_lane, k_ref, vt_ref
        self.acc_ref, self.sum_ref, self.qz_ref, self.k2max_ref, self.bufs, self.tk = (
            acc_ref, sum_ref, qz_ref, k2max_ref, bufs, tk)

    def _keys(self, blk):
        return pl.ds(pl.multiple_of(blk * self.tk, self.tk), self.tk)

    def masked_q(self):
        q = self.q_ref[...].astype(F32)
        return q if self.q_mask is None else q * self.q_mask

    def store_key_norm(self, nblk):
        def blk(b, best):
            k = self.k_ref[self._keys(b), :].astype(F32)
            return jnp.maximum(best, jnp.max(jnp.sum(k * k, axis=1, keepdims=True), axis=0, keepdims=True))
        best = lax.fori_loop(0, nblk, blk, jnp.zeros((1, 1), F32))
        self.k2max_ref[...] = jnp.broadcast_to(best, self.k2max_ref.shape)

    def score_bound(self):
        q = self.masked_q()
        q2 = jnp.max(jnp.sum(q * q, axis=1, keepdims=True), axis=0, keepdims=True)
        return jnp.sqrt(q2 * self.k2max_ref[0:1, 0:1]) * 1.01 + 0.01

    def set_query(self, shift):
        q = self.masked_q()
        if shift is not None:
            lane = lax.broadcasted_iota(jnp.int32, q.shape, 1)
            q = jnp.where(lane == self.bias_lane, -shift, q)
        self.qz_ref[...] = q.astype(BF16)

    def _qk(self, blk):
        return lax.dot_general(self.k_ref[self._keys(blk), :], self.qz_ref[...], (((1,), (1,)), ((), ())),
                               preferred_element_type=F32)

    def _pv(self, blk, p):
        return jnp.dot(self.vt_ref[:, self._keys(blk)], p.astype(BF16), preferred_element_type=F32)

    def scores(self, blk, slot):
        s = self._qk(blk)
        self.bufs[slot][...] = s
        return jnp.max(s, axis=0, keepdims=True)

    def consume(self, blk, slot, blk_max, m, l):
        m_new = jnp.maximum(m, blk_max)
        p = jnp.exp2(self.bufs[slot][...] - m_new)
        alpha = jnp.exp2(m - m_new)
        l_new = alpha * l + jnp.sum(p, axis=0, keepdims=True)
        self.acc_ref[...] = alpha * self.acc_ref[...] + self._pv(blk, p)
        return m_new, l_new

    def accumulate(self, blk, part):
        p = jnp.exp2(self._qk(blk))
        self.acc_ref[...] += self._pv(blk, p)
        return part + jnp.sum(p.reshape(self.tk // 8, 8, p.shape[1]), axis=0)


def _flash_exact(streams, nblk, tq):
    m0 = jnp.full((1, tq), NEG_BIG, F32)
    l0 = jnp.zeros((1, tq), F32)
    for st in streams:
        st.set_query(None)
    first = tuple((m0, l0, st.scores(0, 0)) for st in streams)

    def pair(j, carry):
        b0 = 2 * j
        b2 = jnp.minimum(b0 + 2, nblk - 1)
        out = []
        for st, (m, l, max0) in zip(streams, carry):
            max1 = st.scores(b0 + 1, 1)
            m, l = st.consume(b0, 0, max0, m, l)
            max2 = st.scores(b2, 0)
            m, l = st.consume(b0 + 1, 1, max1, m, l)
            out.append((m, l, max2))
        return tuple(out)

    final = lax.fori_loop(0, nblk // 2, pair, first)
    for st, (_, l, _) in zip(streams, final):
        st.sum_ref[...] = l


def _flash_shifted(streams, bounds, nblk, tq):
    for st, bound in zip(streams, bounds):
        st.set_query(bound)

    def pair(j, parts):
        return tuple(st.accumulate(2 * j + 1, st.accumulate(2 * j, part)) for st, part in zip(streams, parts))

    parts = lax.fori_loop(0, nblk // 2, pair, tuple(jnp.zeros((8, tq), F32) for _ in streams))
    for st, part in zip(streams, parts):
        st.sum_ref[...] = jnp.sum(part, axis=0, keepdims=True)


def _flash(streams, nblk, tq):
    assert nblk % 2 == 0

    @pl.when(pl.program_id(2) == 0)
    def _():
        for st in {id(st.k2max_ref): st for st in streams}.values():
            st.store_key_norm(nblk)

    for st in streams:
        st.acc_ref[...] = jnp.zeros(st.acc_ref.shape, F32)
    bounds = [st.score_bound() for st in streams]
    shift_ok = functools.reduce(jnp.maximum, bounds)[0, 0] <= SHIFT_MAX_BOUND

    @pl.when(shift_ok)
    def _():
        _flash_shifted(streams, bounds, nblk, tq)

    @pl.when(jnp.logical_not(shift_ok))
    def _():
        _flash_exact(streams, nblk, tq)


def _da_body(lam_ref, g_ref, m1_ref, m2_ref, q_ref, k1_ref, k2_ref, vt_ref, o_ref,
             acc_ref, sum_ref, qz_ref, k2max_ref, *bufs, tk, lambda_init):
    tq = q_ref.shape[0]
    masks = (m1_ref[...], m2_ref[...])
    streams = [_Stream(q_ref, masks[j], DA_BIAS_LANES[j], (k1_ref, k2_ref)[j], vt_ref, acc_ref.at[j], sum_ref.at[j],
                       qz_ref.at[j], k2max_ref.at[j], bufs[2 * j:2 * j + 2], tk) for j in range(2)]
    _flash(streams, k1_ref.shape[0] // tk, tq)

    lq1, lk1, lq2, lk2 = (lam_ref[r:r + 1, :] for r in range(4))
    lam = (jnp.exp(jnp.sum(lq1 * lk1, axis=-1, keepdims=True))
           - jnp.exp(jnp.sum(lq2 * lk2, axis=-1, keepdims=True)) + lambda_init)
    ot = acc_ref[0] / sum_ref[0] - lam * (acc_ref[1] / sum_ref[1])
    ot = ot * lax.rsqrt(jnp.mean(ot * ot, axis=0, keepdims=True) + EPS)
    o_ref[...] = (ot.T * g_ref[...] * (1.0 - lambda_init)).astype(o_ref.dtype)


def _mla_body(q_ref, k_ref, vt_ref, o_ref, acc_ref, sum_ref, qz_ref, k2max_ref, *bufs, tk):
    nq = acc_ref.shape[0]
    tq = q_ref.shape[0] // nq
    rows = [pl.ds(i * tq, tq) for i in range(nq)]
    key_norm = k2max_ref.at[0]
    streams = [_Stream(q_ref.at[rows[i], :], None, MLA_BIAS_LANE, k_ref, vt_ref, acc_ref.at[i], sum_ref.at[i],
                       qz_ref.at[i], key_norm, bufs[2 * i:2 * i + 2], tk) for i in range(nq)]
    _flash(streams, k_ref.shape[0] // tk, tq)
    for i in range(nq):
        o_ref[rows[i], :] = (acc_ref[i] / sum_ref[i]).T.astype(o_ref.dtype)


def _attn_specs(batch, heads, seq, qk_w, nq=1):
    tq = min(ATT_TQ, seq // nq)
    tk = min(ATT_TK, seq)
    q_spec = pl.BlockSpec((None, None, nq * tq, qk_w), lambda b, h, i: (b, h, i, 0))
    k_spec = pl.BlockSpec((None, None, seq, qk_w), lambda b, h, i: (b, h, 0, 0))
    vt_spec = pl.BlockSpec((None, None, LANES, seq), lambda b, h, i: (b, h, 0, 0))
    o_spec = pl.BlockSpec((None, nq * tq, LANES), lambda b, h, i: (b, i, h))
    o_shape = jax.ShapeDtypeStruct((batch, seq, heads * LANES), BF16)
    return tq, tk, (batch, heads, seq // (nq * tq)), q_spec, k_spec, vt_spec, o_spec, o_shape


def _attn_scratch(streams, key_sets, tq, tk, qk_w):
    return ([pltpu.VMEM((streams, LANES, tq), F32),
             pltpu.VMEM((streams, 1, tq), F32),
             pltpu.VMEM((streams, tq, qk_w), BF16),
             pltpu.VMEM((key_sets, 8, LANES), F32)]
            + [pltpu.VMEM((tk, tq), F32)] * (2 * streams))


def _da_attn(lam_vecs, subln_g, masks, q, k1, k2, vt, *, lambda_init):
    batch, heads, seq, _ = q.shape
    tq, tk, grid, q_spec, k_spec, vt_spec, o_spec, o_shape = _attn_specs(batch, heads, seq, LANES)
    lane_row = pl.BlockSpec((1, LANES), lambda b, h, i: (0, 0))
    return pl.pallas_call(
        functools.partial(_da_body, tk=tk, lambda_init=lambda_init),
        grid=grid,
        in_specs=[pl.BlockSpec(lam_vecs.shape, lambda b, h, i: (0, 0)), lane_row, lane_row, lane_row,
                  q_spec, k_spec, k_spec, vt_spec],
        out_specs=o_spec,
        out_shape=o_shape,
        scratch_shapes=_attn_scratch(2, 2, tq, tk, LANES),
        compiler_params=_params("arbitrary", "arbitrary", "arbitrary"),
        name="da_attn",
    )(lam_vecs, subln_g, *masks, q, k1, k2, vt)


def _mla_attn(q, k, vt):
    batch, heads, seq, qk_w = q.shape
    nq = MLA_QSTREAMS
    tq, tk, grid, q_spec, k_spec, vt_spec, o_spec, o_shape = _attn_specs(batch, heads, seq, qk_w, nq)
    return pl.pallas_call(
        functools.partial(_mla_body, tk=tk),
        grid=grid,
        in_specs=[q_spec, k_spec, vt_spec],
        out_specs=o_spec,
        out_shape=o_shape,
        scratch_shapes=_attn_scratch(nq, 1, tq, tk, qk_w),
        compiler_params=_params("arbitrary", "arbitrary", "arbitrary"),
        name="mla_attn",
    )(q, k, vt)


def _outproj_body(x_ref, mod_ref, oda_ref, oml_ref, wa_ref, wb_ref, o_ref):
    mix = (jnp.dot(oda_ref[...], wa_ref[...], preferred_element_type=F32)
           + jnp.dot(oml_ref[...], wb_ref[...], preferred_element_type=F32))
    o_ref[...] = x_ref[...] + mod_ref[5:6, :] * mix


def _outproj(x, mod, o_da, o_mla, wo_a, wo_b, *, seq):
    t, d = x.shape
    tm = min(OUT_TM, seq)
    per_seq = seq // tm
    row = lambda i: (i, 0)
    return pl.pallas_call(
        _outproj_body,
        grid=(t // tm,),
        in_specs=[pl.BlockSpec((tm, d), row),
                  pl.BlockSpec((None, N_MOD, d), lambda i: (i // per_seq, 0, 0)),
                  pl.BlockSpec((tm, o_da.shape[1]), row),
                  pl.BlockSpec((tm, o_mla.shape[1]), row),
                  _const_spec(wo_a.shape), _const_spec(wo_b.shape)],
        out_specs=pl.BlockSpec((tm, d), row),
        out_shape=jax.ShapeDtypeStruct((t, d), F32),
        compiler_params=_params("arbitrary"),
        name="outproj",
    )(x, mod, o_da, o_mla, wo_a, wo_b)


def _da_lane_layout():
    src = np.zeros(LANES, np.int32)
    rot = np.full(LANES, -1, np.int32)
    sign = np.zeros(LANES, np.float32)
    mask = np.zeros((2, LANES), np.float32)
    half = DA_ROT // 2
    for j in range(2):
        base = j * DA_QK_DIM
        for i in range(half):
            src[j * half + i] = base + i
            rot[j * half + i] = i
            sign[j * half + i] = -1.0
            src[64 + j * half + i] = base + half + i
            rot[64 + j * half + i] = i
            sign[64 + j * half + i] = 1.0
            mask[j, j * half + i] = mask[j, 64 + j * half + i] = 1.0
        plain = DA_QK_DIM - DA_ROT
        start = DA_ROT + j * 64
        for i in range(plain):
            src[start + i] = base + DA_ROT + i
            mask[j, start + i] = 1.0
    return src, rot, sign, mask


def _mla_lane_layout():
    src = np.full(LANES, -1, np.int32)
    rot = np.full(LANES, -1, np.int32)
    sign = np.zeros(LANES, np.float32)
    half = MLA_ROPE // 2
    for i in range(half):
        src[i], rot[i], sign[i] = i, i, -1.0
        src[64 + i], rot[64 + i], sign[64 + i] = half + i, i, 1.0
    return src, rot, sign


def _take_cols(w, src):
    src = [int(v) for v in src]
    parts, start = [], 0
    for i in range(1, len(src) + 1):
        same_run = i < len(src) and ((src[i] < 0 and src[i - 1] < 0) or (src[i - 1] >= 0 and src[i] == src[i - 1] + 1))
        if not same_run:
            width = i - start
            parts.append(jnp.zeros((w.shape[0], width), w.dtype) if src[start] < 0
                         else w[:, src[start]:src[start] + width])
            start = i
    return jnp.concatenate(parts, axis=1)


def _rope_tables(seq, dim, rot, sign):
    inv = ROPE_THETA ** (-jnp.arange(0, dim, 2, dtype=F32) / dim)
    ang = jnp.arange(seq, dtype=F32)[:, None] * _take_cols(inv[None, :], rot)
    return jnp.cos(ang), jnp.sin(ang) * jnp.asarray(sign)


def _prep_layer(l, ffn1_w1, ffn1_w3, ffn1_w2, w_in, mla_w_uq, mla_w_ukv, w_o, ffn2_w1, ffn2_w3, ffn2_w2):
    da_src, _, _, _ = _da_lane_layout()
    ml_src, _, _ = _mla_lane_layout()
    head_src = np.concatenate([h * 2 * DA_QK_DIM + da_src for h in range(DA_HEADS)])
    o1, o2, o3 = DA_Q_W, 2 * DA_Q_W, 2 * DA_Q_W + DA_V_W
    o5 = o3 + MLA_Q_RANK + MLA_KV_RANK
    wi = w_in[l]
    win = jnp.concatenate([_take_cols(wi[:, :o1], head_src),
                           _take_cols(wi[:, o1:o2], head_src),
                           wi[:, o2:o5],
                           _take_cols(wi[:, o5:], ml_src)], axis=1).astype(BF16)
    per_q = MLA_NOPE + MLA_ROPE
    uq_src = np.concatenate([np.concatenate([h * per_q + np.arange(MLA_NOPE),
                                             np.where(ml_src >= 0, h * per_q + MLA_NOPE + ml_src, -1)])
                             for h in range(MLA_HEADS)])
    per_kv = MLA_NOPE + MLA_V
    ukv_src = np.concatenate([h * per_kv + np.arange(MLA_NOPE) for h in range(MLA_HEADS)]
                             + [h * per_kv + MLA_NOPE + np.arange(MLA_V) for h in range(MLA_HEADS)])
    return dict(
        f1=(ffn1_w1[l].astype(BF16), ffn1_w3[l].astype(BF16), ffn1_w2[l].astype(BF16)),
        f2=(ffn2_w1[l].astype(BF16), ffn2_w3[l].astype(BF16), ffn2_w2[l].astype(BF16)),
        win=win,
        wuq=_take_cols(mla_w_uq[l], uq_src).astype(BF16),
        wukv=_take_cols(mla_w_ukv[l], ukv_src).astype(BF16),
        wo_a=w_o[l][:DA_V_W].astype(BF16),
        wo_b=w_o[l][DA_V_W:].astype(BF16),
    )


def _encode_group(x, mods, prepped, small, final_norm_g, tabs):
    batch, seq, d = x.shape
    xt = x.reshape(batch * seq, d)
    da_mask = _da_lane_layout()[3]
    masks = (jnp.asarray(da_mask[0:1]), jnp.asarray(da_mask[1:2]))
    depth = len(prepped)
    row = lambda v: v.reshape(1, -1)
    for l in range(depth):
        p, s, mod = prepped[l], small[l], mods[l]
        lambda_init = 0.8 - 0.6 * math.exp(-0.3 * l)
        xt = _ffn(xt, mod, row(s["ffn1_norm"]), *p["f1"], row(s["ffn1_norm"]),
                  seq=seq, mod_base=0, final_norm=False)
        qda, k1, k2, vtda, qml, kml, vtml = _inproj(
            xt, mod, row(s["attn_norm"]), p["win"], tabs, masks, row(s["mla_q_norm"]), row(s["mla_kv_norm"]),
            p["wuq"], p["wukv"], batch=batch, seq=seq)
        o_da = _da_attn(s["lam_vecs"], row(s["da_subln"]), masks, qda, k1, k2, vtda, lambda_init=lambda_init)
        o_mla = _mla_attn(qml, kml, vtml)
        xt = _outproj(xt, mod, o_da.reshape(batch * seq, -1), o_mla.reshape(batch * seq, -1),
                      p["wo_a"], p["wo_b"], seq=seq)
        last = l == depth - 1
        xt = _ffn(xt, mod, row(s["ffn2_norm"]), *p["f2"], row(final_norm_g),
                  seq=seq, mod_base=6, final_norm=last)
    return xt.reshape(batch, seq, d)


def kernel(x_prompt, x_sample, c_prompt, c_sample, ffn1_norm, ffn1_w1, ffn1_w3, ffn1_w2, attn_norm, w_in,
           da_lambda_q1, da_lambda_k1, da_lambda_q2, da_lambda_k2, da_subln, mla_q_norm, mla_w_uq, mla_kv_norm,
           mla_w_ukv, w_o, ffn2_norm, ffn2_w1, ffn2_w3, ffn2_w2, w_ada, b_ada, final_norm):
    depth = w_in.shape[0]
    nb_p, nb_s = c_prompt.shape[0], c_sample.shape[0]
    c_all = jnp.concatenate([c_prompt, c_sample], axis=0)
    rows = -(-c_all.shape[0] // 8) * 8
    c_pad = jnp.pad(c_all, ((0, rows - c_all.shape[0]), (0, 0)))

    prepped, small, mods_p, mods_s = [], [], [], []
    for l in range(depth):
        prepped.append(_prep_layer(l, ffn1_w1, ffn1_w3, ffn1_w2, w_in, mla_w_uq, mla_w_ukv, w_o,
                                   ffn2_w1, ffn2_w3, ffn2_w2))
        small.append(dict(
            ffn1_norm=ffn1_norm[l], attn_norm=attn_norm[l], ffn2_norm=ffn2_norm[l], da_subln=da_subln[l],
            mla_q_norm=mla_q_norm[l], mla_kv_norm=mla_kv_norm[l],
            lam_vecs=jnp.stack([da_lambda_q1[l], da_lambda_k1[l], da_lambda_q2[l], da_lambda_k2[l]])))
        m = _ada(c_pad, w_ada[l], b_ada[l].reshape(1, -1))
        mods_p.append(m[:nb_p].reshape(nb_p, N_MOD, D_MODEL))
        mods_s.append(m[nb_p:nb_p + nb_s].reshape(nb_s, N_MOD, D_MODEL))

    _, da_rot, da_sign, _ = _da_lane_layout()
    _, ml_rot, ml_sign = _mla_lane_layout()
    max_seq = max(x_prompt.shape[1], x_sample.shape[1])
    tabs = _rope_tables(max_seq, DA_ROT, da_rot, da_sign) + _rope_tables(max_seq, MLA_ROPE, ml_rot, ml_sign)

    y_prompt = _encode_group(x_prompt, mods_p, prepped, small, final_norm, tabs)
    y_sample = _encode_group(x_sample, mods_s, prepped, small, final_norm, tabs)
    return (y_prompt, y_sample)
```

```python
import functools
import math

import jax
import jax.numpy as jnp
import numpy as np
from jax import lax
from jax.experimental import pallas as pl
from jax.experimental.pallas import tpu as pltpu

D_MODEL = 2048
D_FF = 5632
N_MOD = 9
DA_HEADS = 8
DA_QK_DIM = 64
DA_V_DIM = 128
DA_ROT = 16
DA_Q_W = DA_HEADS * 2 * DA_QK_DIM
DA_V_W = DA_HEADS * DA_V_DIM
MLA_HEADS = 8
MLA_NOPE = 128
MLA_ROPE = 64
MLA_V = 128
MLA_Q_RANK = 512
MLA_KV_RANK = 256
ROPE_THETA = 500000.0
EPS = 1e-6
LANES = 128
MLA_QK_W = 2 * LANES
VMEM_LIMIT = 56 * 1024 * 1024
FFN_VMEM_LIMIT = 60 * 1024 * 1024
NEG_BIG = -1e30
LOG2E = math.log2(math.e)
SHIFT_MAX_BOUND = 40.0
DA_BIAS_LANES = (8, 0)
MLA_BIAS_LANE = LANES + 32

FFN_TM, FFN_TF = 1024, 512
FFN_ROW_CHUNK = 128
PROJ_TM = 256
OUT_TM = 512
ATT_TQ, ATT_TK = 512, 2048
MLA_QSTREAMS = 2
SHIFT_UNROLL = 4
ADA_TN = 1024

BF16 = jnp.bfloat16
F32 = jnp.float32


def _params(*sem, vmem=VMEM_LIMIT):
    return pltpu.CompilerParams(dimension_semantics=sem, vmem_limit_bytes=vmem)


def _rms(x, g):
    return x * lax.rsqrt(jnp.mean(x * x, axis=-1, keepdims=True) + EPS) * g


def _silu(x):
    return x / (1.0 + jnp.exp(-x))


def _const_spec(shape):
    zeros = (0,) * len(shape)
    return pl.BlockSpec(shape, lambda *_: zeros, pipeline_mode=pl.Buffered(1))


def _ada_body(c_ref, w_ref, b_ref, o_ref):
    a = _silu(c_ref[...]).astype(BF16)
    o_ref[...] = jnp.dot(a, w_ref[...].astype(BF16), preferred_element_type=F32) + b_ref[...]


def _ada(c_pad, w, b):
    rows, d = c_pad.shape
    n = w.shape[1]
    return pl.pallas_call(
        _ada_body,
        grid=(n // ADA_TN,),
        in_specs=[pl.BlockSpec((rows, d), lambda j: (0, 0)),
                  pl.BlockSpec((d, ADA_TN), lambda j: (0, j)),
                  pl.BlockSpec((1, ADA_TN), lambda j: (0, j))],
        out_specs=pl.BlockSpec((rows, ADA_TN), lambda j: (0, j)),
        out_shape=jax.ShapeDtypeStruct((rows, n), F32),
        compiler_params=_params("arbitrary"),
        name="ada",
    )(c_pad, w, b)


def _ffn_body(x_ref, mod_ref, g_ref, w1_ref, w3_ref, w2_ref, fg_ref, o_ref, h_ref, *, mod_base, final_norm):
    j = pl.program_id(1)
    last = pl.num_programs(1) - 1
    chunks = [pl.ds(r, FFN_ROW_CHUNK) for r in range(0, x_ref.shape[0], FFN_ROW_CHUNK)]

    def swiglu(h):
        a = jnp.dot(h, w1_ref[...], preferred_element_type=F32)
        b = jnp.dot(h, w3_ref[...], preferred_element_type=F32)
        return jnp.dot((_silu(a) * b).astype(BF16), w2_ref[...], preferred_element_type=F32)

    @pl.when(j == 0)
    def _():
        shift = mod_ref[mod_base:mod_base + 1, :]
        scale = mod_ref[mod_base + 1:mod_base + 2, :]
        for rows in chunks:
            h = (_rms(x_ref[rows, :], g_ref[...]) * (1.0 + scale) + shift).astype(BF16)
            h_ref[rows, :] = h
            o_ref[rows, :] = swiglu(h)

    @pl.when(jnp.logical_and(j > 0, j < last))
    def _():
        o_ref[...] += swiglu(h_ref[...])

    @pl.when(j == last)
    def _():
        gate = mod_ref[mod_base + 2:mod_base + 3, :]
        for rows in chunks:
            y = x_ref[rows, :] + 0.5 * gate * (o_ref[rows, :] + swiglu(h_ref[rows, :]))
            if final_norm:
                y = _rms(y, fg_ref[...])
            o_ref[rows, :] = y


def _ffn(x, mod, g, w1, w3, w2, fg, *, seq, mod_base, final_norm):
    t, d = x.shape
    f = w1.shape[1]
    tm = min(FFN_TM, seq)
    tf = FFN_TF
    assert f // tf >= 2, "the first and the last hidden-dim step must be different steps"
    per_seq = seq // tm
    return pl.pallas_call(
        functools.partial(_ffn_body, mod_base=mod_base, final_norm=final_norm),
        grid=(t // tm, f // tf),
        in_specs=[pl.BlockSpec((tm, d), lambda i, j: (i, 0)),
                  pl.BlockSpec((None, N_MOD, d), lambda i, j: (i // per_seq, 0, 0)),
                  pl.BlockSpec((1, d), lambda i, j: (0, 0)),
                  pl.BlockSpec((d, tf), lambda i, j: (0, j)),
                  pl.BlockSpec((d, tf), lambda i, j: (0, j)),
                  pl.BlockSpec((tf, d), lambda i, j: (j, 0)),
                  pl.BlockSpec((1, d), lambda i, j: (0, 0))],
        out_specs=pl.BlockSpec((tm, d), lambda i, j: (i, 0)),
        out_shape=jax.ShapeDtypeStruct((t, d), F32),
        scratch_shapes=[pltpu.VMEM((tm, d), BF16)],
        compiler_params=_params("arbitrary", "arbitrary", vmem=FFN_VMEM_LIMIT),
        name="ffn_final" if final_norm else "ffn",
    )(x, mod, g, w1, w3, w2, fg)


def _rope(t, cos, sin_signed):
    return t * cos + pltpu.roll(t, LANES // 2, axis=1) * sin_signed


def _inproj_body(x_ref, mod_ref, g_ref, win_ref, cda_ref, sda_ref, cml_ref, sml_ref, m1_ref, m2_ref,
                 gq_ref, gkv_ref, wuq_ref, wukv_ref,
                 qda_ref, k1_ref, k2_ref, vtda_ref, qml_ref, kml_ref, vtml_ref):
    shift = mod_ref[3:4, :]
    scale = mod_ref[4:5, :]
    h = (_rms(x_ref[...], g_ref[...]) * (1.0 + scale) + shift).astype(BF16)
    cda, sda = cda_ref[...], sda_ref[...]
    cml, sml = cml_ref[...], sml_ref[...]
    mask1, mask2 = m1_ref[...], m2_ref[...]
    da_scale = DA_QK_DIM ** -0.5 * LOG2E
    mla_scale = (MLA_NOPE + MLA_ROPE) ** -0.5 * LOG2E

    def tile(a, k):
        return a[:, k * LANES:(k + 1) * LANES]

    def one_hot(lane):
        return (lax.broadcasted_iota(jnp.int32, (1, LANES), 1) == lane).astype(F32)

    pq = jnp.dot(h, win_ref[:, 0:DA_Q_W], preferred_element_type=F32)
    for hd in range(DA_HEADS):
        qda_ref[hd] = (_rope(tile(pq, hd), cda, sda) * da_scale).astype(BF16)
    pk = jnp.dot(h, win_ref[:, DA_Q_W:2 * DA_Q_W], preferred_element_type=F32)
    for hd in range(DA_HEADS):
        r = _rope(tile(pk, hd), cda, sda)
        k1_ref[hd] = (r * mask1 + one_hot(DA_BIAS_LANES[0])).astype(BF16)
        k2_ref[hd] = (r * mask2 + one_hot(DA_BIAS_LANES[1])).astype(BF16)
    pv = jnp.dot(h, win_ref[:, 2 * DA_Q_W:2 * DA_Q_W + DA_V_W], preferred_element_type=F32)
    for hd in range(DA_HEADS):
        vtda_ref[hd] = tile(pv, hd).T.astype(BF16)

    o3 = 2 * DA_Q_W + DA_V_W
    pr = jnp.dot(h, win_ref[:, o3:], preferred_element_type=F32)
    cq = _rms(pr[:, 0:MLA_Q_RANK], gq_ref[...]).astype(BF16)
    ckv = _rms(pr[:, MLA_Q_RANK:MLA_Q_RANK + MLA_KV_RANK], gkv_ref[...]).astype(BF16)
    kpe = (_rope(pr[:, MLA_Q_RANK + MLA_KV_RANK:], cml, sml) + one_hot(MLA_BIAS_LANE - LANES)).astype(BF16)

    qm = jnp.dot(cq, wuq_ref[...], preferred_element_type=F32)
    for hd in range(MLA_HEADS):
        qml_ref[hd, :, 0:LANES] = (tile(qm, 2 * hd) * mla_scale).astype(BF16)
        qml_ref[hd, :, LANES:] = (_rope(tile(qm, 2 * hd + 1), cml, sml) * mla_scale).astype(BF16)
    kv = jnp.dot(ckv, wukv_ref[...], preferred_element_type=F32)
    for hd in range(MLA_HEADS):
        kml_ref[hd, :, 0:LANES] = tile(kv, hd).astype(BF16)
        kml_ref[hd, :, LANES:] = kpe
        vtml_ref[hd] = tile(kv, MLA_HEADS + hd).T.astype(BF16)


def _inproj(x, mod, g, win, tabs, masks, gq, gkv, wuq, wukv, *, batch, seq):
    t, d = x.shape
    tm = min(PROJ_TM, seq)
    per_seq = seq // tm
    row = lambda i: (i, 0)
    tab = lambda i: (i % per_seq, 0)
    head_rows = lambda i: (i // per_seq, 0, i % per_seq, 0)
    head_cols = lambda i: (i // per_seq, 0, 0, i % per_seq)
    hq = lambda w: pl.BlockSpec((None, DA_HEADS, tm, w), head_rows)
    vt = pl.BlockSpec((None, DA_HEADS, LANES, tm), head_cols)
    sd = jax.ShapeDtypeStruct
    return pl.pallas_call(
        _inproj_body,
        grid=(t // tm,),
        in_specs=[pl.BlockSpec((tm, d), row),
                  pl.BlockSpec((None, N_MOD, d), lambda i: (i // per_seq, 0, 0)),
                  _const_spec((1, d)),
                  _const_spec(win.shape),
                  pl.BlockSpec((tm, LANES), tab), pl.BlockSpec((tm, LANES), tab),
                  pl.BlockSpec((tm, LANES), tab), pl.BlockSpec((tm, LANES), tab),
                  _const_spec((1, LANES)), _const_spec((1, LANES)),
                  _const_spec((1, MLA_Q_RANK)), _const_spec((1, MLA_KV_RANK)),
                  _const_spec(wuq.shape), _const_spec(wukv.shape)],
        out_specs=[hq(LANES), hq(LANES), hq(LANES), vt, hq(MLA_QK_W), hq(MLA_QK_W), vt],
        out_shape=[sd((batch, DA_HEADS, seq, LANES), BF16), sd((batch, DA_HEADS, seq, LANES), BF16),
                   sd((batch, DA_HEADS, seq, LANES), BF16), sd((batch, DA_HEADS, LANES, seq), BF16),
                   sd((batch, MLA_HEADS, seq, MLA_QK_W), BF16), sd((batch, MLA_HEADS, seq, MLA_QK_W), BF16),
                   sd((batch, MLA_HEADS, LANES, seq), BF16)],
        compiler_params=_params("arbitrary"),
        name="inproj",
    )(x, mod, g, win, *tabs, *masks, gq, gkv, wuq, wukv)


class _Stream:
    def __init__(self, q_ref, q_mask, bias_lane, k_ref, vt_ref, acc_ref, sum_ref, qz_ref, k2max_ref, bufs, tk):
        self.q_ref, self.q_mask, self.bias_lane, self.k_ref, self.vt_ref = q_ref, q_mask, bias_lane, k_ref, vt_ref
        self.acc_ref, self.sum_ref, self.qz_ref, self.k2max_ref, self.bufs, self.tk = (
            acc_ref, sum_ref, qz_ref, k2max_ref, bufs, tk)

    def _keys(self, blk):
        return pl.ds(pl.multiple_of(blk * self.tk, self.tk), self.tk)

    def masked_q(self):
        q = self.q_ref[...].astype(F32)
        return q if self.q_mask is None else q * self.q_mask

    def store_key_norm(self, nblk):
        def blk(b, best):
            k = self.k_ref[self._keys(b), :].astype(F32)
            return jnp.maximum(best, jnp.max(jnp.sum(k * k, axis=1, keepdims=True), axis=0, keepdims=True))
        best = lax.fori_loop(0, nblk, blk, jnp.zeros((1, 1), F32))
        self.k2max_ref[...] = jnp.broadcast_to(best, self.k2max_ref.shape)

    def score_bound(self):
        q = self.masked_q()
        q2 = jnp.max(jnp.sum(q * q, axis=1, keepdims=True), axis=0, keepdims=True)
        return jnp.sqrt(q2 * self.k2max_ref[0:1, 0:1]) * 1.01 + 0.01

    def set_query(self, shift):
        q = self.masked_q()
        if shift is not None:
            lane = lax.broadcasted_iota(jnp.int32, q.shape, 1)
            q = jnp.where(lane == self.bias_lane, -shift, q)
        self.qz_ref[...] = q.astype(BF16)

    def _qk(self, blk):
        return lax.dot_general(self.k_ref[self._keys(blk), :], self.qz_ref[...], (((1,), (1,)), ((), ())),
                               preferred_element_type=F32)

    def _pv(self, blk, p):
        return jnp.dot(self.vt_ref[:, self._keys(blk)], p.astype(BF16), preferred_element_type=F32)

    def scores(self, blk, slot):
        s = self._qk(blk)
        self.bufs[slot][...] = s
        return jnp.max(s, axis=0, keepdims=True)

    def consume(self, blk, slot, blk_max, m, l):
        m_new = jnp.maximum(m, blk_max)
        p = jnp.exp2(self.bufs[slot][...] - m_new)
        alpha = jnp.exp2(m - m_new)
        l_new = alpha * l + jnp.sum(p, axis=0, keepdims=True)
        self.acc_ref[...] = alpha * self.acc_ref[...] + self._pv(blk, p)
        return m_new, l_new

    def accumulate(self, blk, part):
        p = jnp.exp2(self._qk(blk))
        self.acc_ref[...] += self._pv(blk, p)
        return part + jnp.sum(p.reshape(self.tk // 8, 8, p.shape[1]), axis=0)


def _flash_exact(streams, nblk, tq):
    m0 = jnp.full((1, tq), NEG_BIG, F32)
    l0 = jnp.zeros((1, tq), F32)
    for st in streams:
        st.set_query(None)
    first = tuple((m0, l0, st.scores(0, 0)) for st in streams)

    def pair(j, carry):
        b0 = 2 * j
        b2 = jnp.minimum(b0 + 2, nblk - 1)
        out = []
        for st, (m, l, max0) in zip(streams, carry):
            max1 = st.scores(b0 + 1, 1)
            m, l = st.consume(b0, 0, max0, m, l)
            max2 = st.scores(b2, 0)
            m, l = st.consume(b0 + 1, 1, max1, m, l)
            out.append((m, l, max2))
        return tuple(out)

    final = lax.fori_loop(0, nblk // 2, pair, first)
    for st, (_, l, _) in zip(streams, final):
        st.sum_ref[...] = l


def _flash_shifted(streams, bounds, nblk, tq):
    for st, bound in zip(streams, bounds):
        st.set_query(bound)

    unroll = SHIFT_UNROLL if nblk % SHIFT_UNROLL == 0 else 2

    def group(j, parts):
        for u in range(unroll):
            parts = tuple(st.accumulate(unroll * j + u, part) for st, part in zip(streams, parts))
        return parts

    parts = lax.fori_loop(0, nblk // unroll, group, tuple(jnp.zeros((8, tq), F32) for _ in streams))
    for st, part in zip(streams, parts):
        st.sum_ref[...] = jnp.sum(part, axis=0, keepdims=True)


def _flash(streams, nblk, tq):
    assert nblk % 2 == 0

    @pl.when(pl.program_id(2) == 0)
    def _():
        for st in {id(st.k2max_ref): st for st in streams}.values():
            st.store_key_norm(nblk)

    for st in streams:
        st.acc_ref[...] = jnp.zeros(st.acc_ref.shape, F32)
    bounds = [st.score_bound() for st in streams]
    shift_ok = functools.reduce(jnp.maximum, bounds)[0, 0] <= SHIFT_MAX_BOUND

    @pl.when(shift_ok)
    def _():
        _flash_shifted(streams, bounds, nblk, tq)

    @pl.when(jnp.logical_not(shift_ok))
    def _():
        _flash_exact(streams, nblk, tq)


def _da_body(lam_ref, g_ref, m1_ref, m2_ref, q_ref, k1_ref, k2_ref, vt_ref, o_ref,
             acc_ref, sum_ref, qz_ref, k2max_ref, *bufs, tk, lambda_init):
    tq = q_ref.shape[0]
    masks = (m1_ref[...], m2_ref[...])
    lam_scr = bufs[-1]
    streams = [_Stream(q_ref, masks[j], DA_BIAS_LANES[j], (k1_ref, k2_ref)[j], vt_ref, acc_ref.at[j], sum_ref.at[j],
                       qz_ref.at[j], k2max_ref.at[j], bufs[2 * j:2 * j + 2], tk) for j in range(2)]

    @pl.when(pl.program_id(2) == 0)
    def _():
        lq1, lk1, lq2, lk2 = (lam_ref[r:r + 1, :] for r in range(4))
        lam = (jnp.exp(jnp.sum(lq1 * lk1, axis=-1, keepdims=True))
               - jnp.exp(jnp.sum(lq2 * lk2, axis=-1, keepdims=True)) + lambda_init)
        lam_scr[...] = jnp.broadcast_to(lam, lam_scr.shape)

    _flash(streams, k1_ref.shape[0] // tk, tq)
    lam = lam_scr[0:1, 0:1]
    ot = acc_ref[0] / sum_ref[0] - lam * (acc_ref[1] / sum_ref[1])
    ot = ot * lax.rsqrt(jnp.mean(ot * ot, axis=0, keepdims=True) + EPS)
    o_ref[...] = (ot.T * g_ref[...] * (1.0 - lambda_init)).astype(o_ref.dtype)


def _mla_body(q_ref, k_ref, vt_ref, o_ref, acc_ref, sum_ref, qz_ref, k2max_ref, *bufs, tk):
    nq = acc_ref.shape[0]
    tq = q_ref.shape[0] // nq
    rows = [pl.ds(i * tq, tq) for i in range(nq)]
    key_norm = k2max_ref.at[0]
    streams = [_Stream(q_ref.at[rows[i], :], None, MLA_BIAS_LANE, k_ref, vt_ref, acc_ref.at[i], sum_ref.at[i],
                       qz_ref.at[i], key_norm, bufs[2 * i:2 * i + 2], tk) for i in range(nq)]
    _flash(streams, k_ref.shape[0] // tk, tq)
    for i in range(nq):
        o_ref[rows[i], :] = (acc_ref[i] / sum_ref[i]).T.astype(o_ref.dtype)


def _attn_specs(batch, heads, seq, qk_w, nq=1):
    tq = min(ATT_TQ, seq // nq)
    tk = min(ATT_TK, seq)
    q_spec = pl.BlockSpec((None, None, nq * tq, qk_w), lambda b, h, i: (b, h, i, 0))
    k_spec = pl.BlockSpec((None, None, seq, qk_w), lambda b, h, i: (b, h, 0, 0))
    vt_spec = pl.BlockSpec((None, None, LANES, seq), lambda b, h, i: (b, h, 0, 0))
    o_spec = pl.BlockSpec((None, nq * tq, LANES), lambda b, h, i: (b, i, h))
    o_shape = jax.ShapeDtypeStruct((batch, seq, heads * LANES), BF16)
    return tq, tk, (batch, heads, seq // (nq * tq)), q_spec, k_spec, vt_spec, o_spec, o_shape


def _attn_scratch(streams, key_sets, tq, tk, qk_w):
    return ([pltpu.VMEM((streams, LANES, tq), F32),
             pltpu.VMEM((streams, 1, tq), F32),
             pltpu.VMEM((streams, tq, qk_w), BF16),
             pltpu.VMEM((key_sets, 8, LANES), F32)]
            + [pltpu.VMEM((tk, tq), F32)] * (2 * streams))


def _da_attn(lam_vecs, subln_g, masks, q, k1, k2, vt, *, lambda_init):
    batch, heads, seq, _ = q.shape
    tq, tk, grid, q_spec, k_spec, vt_spec, o_spec, o_shape = _attn_specs(batch, heads, seq, LANES)
    lane_row = pl.BlockSpec((1, LANES), lambda b, h, i: (0, 0))
    return pl.pallas_call(
        functools.partial(_da_body, tk=tk, lambda_init=lambda_init),
        grid=grid,
        in_specs=[pl.BlockSpec(lam_vecs.shape, lambda b, h, i: (0, 0)), lane_row, lane_row, lane_row,
                  q_spec, k_spec, k_spec, vt_spec],
        out_specs=o_spec,
        out_shape=o_shape,
        scratch_shapes=_attn_scratch(2, 2, tq, tk, LANES) + [pltpu.VMEM((8, LANES), F32)],
        compiler_params=_params("arbitrary", "arbitrary", "arbitrary"),
        name="da_attn",
    )(lam_vecs, subln_g, *masks, q, k1, k2, vt)


def _mla_attn(q, k, vt):
    batch, heads, seq, qk_w = q.shape
    nq = MLA_QSTREAMS
    tq, tk, grid, q_spec, k_spec, vt_spec, o_spec, o_shape = _attn_specs(batch, heads, seq, qk_w, nq)
    return pl.pallas_call(
        functools.partial(_mla_body, tk=tk),
        grid=grid,
        in_specs=[q_spec, k_spec, vt_spec],
        out_specs=o_spec,
        out_shape=o_shape,
        scratch_shapes=_attn_scratch(nq, 1, tq, tk, qk_w),
        compiler_params=_params("arbitrary", "arbitrary", "arbitrary"),
        name="mla_attn",
    )(q, k, vt)


def _outproj_body(x_ref, mod_ref, oda_ref, oml_ref, wa_ref, wb_ref, o_ref):
    mix = (jnp.dot(oda_ref[...], wa_ref[...], preferred_element_type=F32)
           + jnp.dot(oml_ref[...], wb_ref[...], preferred_element_type=F32))
    o_ref[...] = x_ref[...] + mod_ref[5:6, :] * mix


def _outproj(x, mod, o_da, o_mla, wo_a, wo_b, *, seq):
    t, d = x.shape
    tm = min(OUT_TM, seq)
    per_seq = seq // tm
    row = lambda i: (i, 0)
    return pl.pallas_call(
        _outproj_body,
        grid=(t // tm,),
        in_specs=[pl.BlockSpec((tm, d), row),
                  pl.BlockSpec((None, N_MOD, d), lambda i: (i // per_seq, 0, 0)),
                  pl.BlockSpec((tm, o_da.shape[1]), row),
                  pl.BlockSpec((tm, o_mla.shape[1]), row),
                  _const_spec(wo_a.shape), _const_spec(wo_b.shape)],
        out_specs=pl.BlockSpec((tm, d), row),
        out_shape=jax.ShapeDtypeStruct((t, d), F32),
        compiler_params=_params("arbitrary"),
        name="outproj",
    )(x, mod, o_da, o_mla, wo_a, wo_b)


def _da_lane_layout():
    src = np.zeros(LANES, np.int32)
    rot = np.full(LANES, -1, np.int32)
    sign = np.zeros(LANES, np.float32)
    mask = np.zeros((2, LANES), np.float32)
    half = DA_ROT // 2
    for j in range(2):
        base = j * DA_QK_DIM
        for i in range(half):
            src[j * half + i] = base + i
            rot[j * half + i] = i
            sign[j * half + i] = -1.0
            src[64 + j * half + i] = base + half + i
            rot[64 + j * half + i] = i
            sign[64 + j * half + i] = 1.0
            mask[j, j * half + i] = mask[j, 64 + j * half + i] = 1.0
        plain = DA_QK_DIM - DA_ROT
        start = DA_ROT + j * 64
        for i in range(plain):
            src[start + i] = base + DA_ROT + i
            mask[j, start + i] = 1.0
    return src, rot, sign, mask


def _mla_lane_layout():
    src = np.full(LANES, -1, np.int32)
    rot = np.full(LANES, -1, np.int32)
    sign = np.zeros(LANES, np.float32)
    half = MLA_ROPE // 2
    for i in range(half):
        src[i], rot[i], sign[i] = i, i, -1.0
        src[64 + i], rot[64 + i], sign[64 + i] = half + i, i, 1.0
    return src, rot, sign


def _take_cols(w, src):
    src = [int(v) for v in src]
    parts, start = [], 0
    for i in range(1, len(src) + 1):
        same_run = i < len(src) and ((src[i] < 0 and src[i - 1] < 0) or (src[i - 1] >= 0 and src[i] == src[i - 1] + 1))
        if not same_run:
            width = i - start
            parts.append(jnp.zeros((w.shape[0], width), w.dtype) if src[start] < 0
                         else w[:, src[start]:src[start] + width])
            start = i
    return jnp.concatenate(parts, axis=1)


def _rope_tables(seq, dim, rot, sign):
    inv = ROPE_THETA ** (-jnp.arange(0, dim, 2, dtype=F32) / dim)
    ang = jnp.arange(seq, dtype=F32)[:, None] * _take_cols(inv[None, :], rot)
    return jnp.cos(ang), jnp.sin(ang) * jnp.asarray(sign)


def _prep_layer(l, ffn1_w1, ffn1_w3, ffn1_w2, w_in, mla_w_uq, mla_w_ukv, w_o, ffn2_w1, ffn2_w3, ffn2_w2):
    da_src, _, _, _ = _da_lane_layout()
    ml_src, _, _ = _mla_lane_layout()
    head_src = np.concatenate([h * 2 * DA_QK_DIM + da_src for h in range(DA_HEADS)])
    o1, o2, o3 = DA_Q_W, 2 * DA_Q_W, 2 * DA_Q_W + DA_V_W
    o5 = o3 + MLA_Q_RANK + MLA_KV_RANK
    wi = w_in[l]
    win = jnp.concatenate([_take_cols(wi[:, :o1], head_src),
                           _take_cols(wi[:, o1:o2], head_src),
                           wi[:, o2:o5],
                           _take_cols(wi[:, o5:], ml_src)], axis=1).astype(BF16)
    per_q = MLA_NOPE + MLA_ROPE
    uq_src = np.concatenate([np.concatenate([h * per_q + np.arange(MLA_NOPE),
                                             np.where(ml_src >= 0, h * per_q + MLA_NOPE + ml_src, -1)])
                             for h in range(MLA_HEADS)])
    per_kv = MLA_NOPE + MLA_V
    ukv_src = np.concatenate([h * per_kv + np.arange(MLA_NOPE) for h in range(MLA_HEADS)]
                             + [h * per_kv + MLA_NOPE + np.arange(MLA_V) for h in range(MLA_HEADS)])
    return dict(
        f1=(ffn1_w1[l].astype(BF16), ffn1_w3[l].astype(BF16), ffn1_w2[l].astype(BF16)),
        f2=(ffn2_w1[l].astype(BF16), ffn2_w3[l].astype(BF16), ffn2_w2[l].astype(BF16)),
        win=win,
        wuq=_take_cols(mla_w_uq[l], uq_src).astype(BF16),
        wukv=_take_cols(mla_w_ukv[l], ukv_src).astype(BF16),
        wo_a=w_o[l][:DA_V_W].astype(BF16),
        wo_b=w_o[l][DA_V_W:].astype(BF16),
    )


def _encode_group(x, mods, prepped, small, final_norm_g, tabs):
    batch, seq, d = x.shape
    xt = x.reshape(batch * seq, d)
    da_mask = _da_lane_layout()[3]
    masks = (jnp.asarray(da_mask[0:1]), jnp.asarray(da_mask[1:2]))
    depth = len(prepped)
    row = lambda v: v.reshape(1, -1)
    for l in range(depth):
        p, s, mod = prepped[l], small[l], mods[l]
        lambda_init = 0.8 - 0.6 * math.exp(-0.3 * l)
        xt = _ffn(xt, mod, row(s["ffn1_norm"]), *p["f1"], row(s["ffn1_norm"]),
                  seq=seq, mod_base=0, final_norm=False)
        qda, k1, k2, vtda, qml, kml, vtml = _inproj(
            xt, mod, row(s["attn_norm"]), p["win"], tabs, masks, row(s["mla_q_norm"]), row(s["mla_kv_norm"]),
            p["wuq"], p["wukv"], batch=batch, seq=seq)
        o_da = _da_attn(s["lam_vecs"], row(s["da_subln"]), masks, qda, k1, k2, vtda, lambda_init=lambda_init)
        o_mla = _mla_attn(qml, kml, vtml)
        xt = _outproj(xt, mod, o_da.reshape(batch * seq, -1), o_mla.reshape(batch * seq, -1),
                      p["wo_a"], p["wo_b"], seq=seq)
        last = l == depth - 1
        xt = _ffn(xt, mod, row(s["ffn2_norm"]), *p["f2"], row(final_norm_g),
                  seq=seq, mod_base=6, final_norm=last)
    return xt.reshape(batch, seq, d)


def kernel(x_prompt, x_sample, c_prompt, c_sample, ffn1_norm, ffn1_w1, ffn1_w3, ffn1_w2, attn_norm, w_in,
           da_lambda_q1, da_lambda_k1, da_lambda_q2, da_lambda_k2, da_subln, mla_q_norm, mla_w_uq, mla_kv_norm,
           mla_w_ukv, w_o, ffn2_norm, ffn2_w1, ffn2_w3, ffn2_w2, w_ada, b_ada, final_norm):
    depth = w_in.shape[0]
    nb_p, nb_s = c_prompt.shape[0], c_sample.shape[0]
    c_all = jnp.concatenate([c_prompt, c_sample], axis=0)
    rows = -(-c_all.shape[0] // 8) * 8
    c_pad = jnp.pad(c_all, ((0, rows - c_all.shape[0]), (0, 0)))

    prepped, small, mods_p, mods_s = [], [], [], []
    for l in range(depth):
        prepped.append(_prep_layer(l, ffn1_w1, ffn1_w3, ffn1_w2, w_in, mla_w_uq, mla_w_ukv, w_o,
                                   ffn2_w1, ffn2_w3, ffn2_w2))
        small.append(dict(
            ffn1_norm=ffn1_norm[l], attn_norm=attn_norm[l], ffn2_norm=ffn2_norm[l], da_subln=da_subln[l],
            mla_q_norm=mla_q_norm[l], mla_kv_norm=mla_kv_norm[l],
            lam_vecs=jnp.stack([da_lambda_q1[l], da_lambda_k1[l], da_lambda_q2[l], da_lambda_k2[l]])))
        m = _ada(c_pad, w_ada[l], b_ada[l].reshape(1, -1))
        mods_p.append(m[:nb_p].reshape(nb_p, N_MOD, D_MODEL))
        mods_s.append(m[nb_p:nb_p + nb_s].reshape(nb_s, N_MOD, D_MODEL))

    _, da_rot, da_sign, _ = _da_lane_layout()
    _, ml_rot, ml_sign = _mla_lane_layout()
    max_seq = max(x_prompt.shape[1], x_sample.shape[1])
    tabs = _rope_tables(max_seq, DA_ROT, da_rot, da_sign) + _rope_tables(max_seq, MLA_ROPE, ml_rot, ml_sign)

    y_prompt = _encode_group(x_prompt, mods_p, prepped, small, final_norm, tabs)
    y_sample = _encode_group(x_sample, mods_s, prepped, small, final_norm, tabs)
    return (y_prompt, y_sample)
```

```python
import functools
import math

import jax
import jax.numpy as jnp
import numpy as np
from jax import lax
from jax.experimental import pallas as pl
from jax.experimental.pallas import tpu as pltpu

D_MODEL = 2048
D_FF = 5632
N_MOD = 9
DA_HEADS = 8
DA_QK_DIM = 64
DA_V_DIM = 128
DA_ROT = 16
DA_Q_W = DA_HEADS * 2 * DA_QK_DIM
DA_V_W = DA_HEADS * DA_V_DIM
MLA_HEADS = 8
MLA_NOPE = 128
MLA_ROPE = 64
MLA_V = 128
MLA_Q_RANK = 512
MLA_KV_RANK = 256
ROPE_THETA = 500000.0
EPS = 1e-6
LANES = 128
MLA_QK_W = 2 * LANES
VMEM_LIMIT = 56 * 1024 * 1024
FFN_VMEM_LIMIT = 60 * 1024 * 1024
NEG_BIG = -1e30
LOG2E = math.log2(math.e)
SHIFT_MAX_BOUND = 40.0
DA_BIAS_LANES = (8, 0)
MLA_BIAS_LANE = LANES + 32

FFN_TM, FFN_TF = 1024, 512
FFN_ROW_CHUNK = 256
PROJ_TM = 256
OUT_TM = 512
ATT_TQ, ATT_TK = 512, 2048
MLA_QSTREAMS = 2
ATT_QSUB = 2
SHIFT_UNROLL = 4
ADA_TN = 1024

BF16 = jnp.bfloat16
F32 = jnp.float32


def _params(*sem, vmem=VMEM_LIMIT):
    return pltpu.CompilerParams(dimension_semantics=sem, vmem_limit_bytes=vmem)


def _rms(x, g):
    return x * lax.rsqrt(jnp.mean(x * x, axis=-1, keepdims=True) + EPS) * g


def _silu(x):
    return x / (1.0 + jnp.exp(-x))


def _const_spec(shape):
    zeros = (0,) * len(shape)
    return pl.BlockSpec(shape, lambda *_: zeros, pipeline_mode=pl.Buffered(1))


def _ada_body(c_ref, w_ref, b_ref, o_ref):
    a = _silu(c_ref[...]).astype(BF16)
    o_ref[...] = jnp.dot(a, w_ref[...].astype(BF16), preferred_element_type=F32) + b_ref[...]


def _ada(c_pad, w, b):
    rows, d = c_pad.shape
    n = w.shape[1]
    return pl.pallas_call(
        _ada_body,
        grid=(n // ADA_TN,),
        in_specs=[pl.BlockSpec((rows, d), lambda j: (0, 0)),
                  pl.BlockSpec((d, ADA_TN), lambda j: (0, j)),
                  pl.BlockSpec((1, ADA_TN), lambda j: (0, j))],
        out_specs=pl.BlockSpec((rows, ADA_TN), lambda j: (0, j)),
        out_shape=jax.ShapeDtypeStruct((rows, n), F32),
        compiler_params=_params("arbitrary"),
        name="ada",
    )(c_pad, w, b)


def _ffn_body(x_ref, mod_ref, g_ref, w1_ref, w3_ref, w2_ref, fg_ref, o_ref, h_ref, *, mod_base, final_norm):
    j = pl.program_id(1)
    last = pl.num_programs(1) - 1
    chunks = [pl.ds(r, FFN_ROW_CHUNK) for r in range(0, x_ref.shape[0], FFN_ROW_CHUNK)]

    def swiglu(h):
        a = jnp.dot(h, w1_ref[...], preferred_element_type=F32)
        b = jnp.dot(h, w3_ref[...], preferred_element_type=F32)
        return jnp.dot((_silu(a) * b).astype(BF16), w2_ref[...], preferred_element_type=F32)

    @pl.when(j == 0)
    def _():
        shift = mod_ref[mod_base:mod_base + 1, :]
        scale = mod_ref[mod_base + 1:mod_base + 2, :]
        for rows in chunks:
            h = (_rms(x_ref[rows, :], g_ref[...]) * (1.0 + scale) + shift).astype(BF16)
            h_ref[rows, :] = h
            o_ref[rows, :] = swiglu(h)

    @pl.when(jnp.logical_and(j > 0, j < last))
    def _():
        o_ref[...] += swiglu(h_ref[...])

    @pl.when(j == last)
    def _():
        gate = mod_ref[mod_base + 2:mod_base + 3, :]
        for rows in chunks:
            y = x_ref[rows, :] + 0.5 * gate * (o_ref[rows, :] + swiglu(h_ref[rows, :]))
            if final_norm:
                y = _rms(y, fg_ref[...])
            o_ref[rows, :] = y


def _ffn(x, mod, g, w1, w3, w2, fg, *, seq, mod_base, final_norm):
    t, d = x.shape
    f = w1.shape[1]
    tm = min(FFN_TM, seq)
    tf = FFN_TF
    assert f // tf >= 2, "the first and the last hidden-dim step must be different steps"
    per_seq = seq // tm
    return pl.pallas_call(
        functools.partial(_ffn_body, mod_base=mod_base, final_norm=final_norm),
        grid=(t // tm, f // tf),
        in_specs=[pl.BlockSpec((tm, d), lambda i, j: (i, 0)),
                  pl.BlockSpec((None, N_MOD, d), lambda i, j: (i // per_seq, 0, 0)),
                  pl.BlockSpec((1, d), lambda i, j: (0, 0)),
                  pl.BlockSpec((d, tf), lambda i, j: (0, j)),
                  pl.BlockSpec((d, tf), lambda i, j: (0, j)),
                  pl.BlockSpec((tf, d), lambda i, j: (j, 0)),
                  pl.BlockSpec((1, d), lambda i, j: (0, 0))],
        out_specs=pl.BlockSpec((tm, d), lambda i, j: (i, 0)),
        out_shape=jax.ShapeDtypeStruct((t, d), F32),
        scratch_shapes=[pltpu.VMEM((tm, d), BF16)],
        compiler_params=_params("arbitrary", "arbitrary", vmem=FFN_VMEM_LIMIT),
        name="ffn_final" if final_norm else "ffn",
    )(x, mod, g, w1, w3, w2, fg)


def _rope(t, cos, sin_signed):
    return t * cos + pltpu.roll(t, LANES // 2, axis=1) * sin_signed


def _inproj_body(x_ref, mod_ref, g_ref, win_ref, cda_ref, sda_ref, cml_ref, sml_ref, m1_ref, m2_ref,
                 gq_ref, gkv_ref, wuq_ref, wukv_ref,
                 qda_ref, k1_ref, k2_ref, vtda_ref, qml_ref, kml_ref, vtml_ref):
    shift = mod_ref[3:4, :]
    scale = mod_ref[4:5, :]
    h = (_rms(x_ref[...], g_ref[...]) * (1.0 + scale) + shift).astype(BF16)
    cda, sda = cda_ref[...], sda_ref[...]
    cml, sml = cml_ref[...], sml_ref[...]
    mask1, mask2 = m1_ref[...], m2_ref[...]
    da_scale = DA_QK_DIM ** -0.5 * LOG2E
    mla_scale = (MLA_NOPE + MLA_ROPE) ** -0.5 * LOG2E

    def tile(a, k):
        return a[:, k * LANES:(k + 1) * LANES]

    def one_hot(lane):
        return (lax.broadcasted_iota(jnp.int32, (1, LANES), 1) == lane).astype(F32)

    pq = jnp.dot(h, win_ref[:, 0:DA_Q_W], preferred_element_type=F32)
    for hd in range(DA_HEADS):
        qda_ref[hd] = (_rope(tile(pq, hd), cda, sda) * da_scale).astype(BF16)
    pk = jnp.dot(h, win_ref[:, DA_Q_W:2 * DA_Q_W], preferred_element_type=F32)
    for hd in range(DA_HEADS):
        r = _rope(tile(pk, hd), cda, sda)
        k1_ref[hd] = (r * mask1 + one_hot(DA_BIAS_LANES[0])).astype(BF16)
        k2_ref[hd] = (r * mask2 + one_hot(DA_BIAS_LANES[1])).astype(BF16)
    pv = jnp.dot(h, win_ref[:, 2 * DA_Q_W:2 * DA_Q_W + DA_V_W], preferred_element_type=F32)
    for hd in range(DA_HEADS):
        vtda_ref[hd] = tile(pv, hd).T.astype(BF16)

    o3 = 2 * DA_Q_W + DA_V_W
    pr = jnp.dot(h, win_ref[:, o3:], preferred_element_type=F32)
    cq = _rms(pr[:, 0:MLA_Q_RANK], gq_ref[...]).astype(BF16)
    ckv = _rms(pr[:, MLA_Q_RANK:MLA_Q_RANK + MLA_KV_RANK], gkv_ref[...]).astype(BF16)
    kpe = (_rope(pr[:, MLA_Q_RANK + MLA_KV_RANK:], cml, sml) + one_hot(MLA_BIAS_LANE - LANES)).astype(BF16)

    qm = jnp.dot(cq, wuq_ref[...], preferred_element_type=F32)
    for hd in range(MLA_HEADS):
        qml_ref[hd, :, 0:LANES] = (tile(qm, 2 * hd) * mla_scale).astype(BF16)
        qml_ref[hd, :, LANES:] = (_rope(tile(qm, 2 * hd + 1), cml, sml) * mla_scale).astype(BF16)
    kv = jnp.dot(ckv, wukv_ref[...], preferred_element_type=F32)
    for hd in range(MLA_HEADS):
        kml_ref[hd, :, 0:LANES] = tile(kv, hd).astype(BF16)
        kml_ref[hd, :, LANES:] = kpe
        vtml_ref[hd] = tile(kv, MLA_HEADS + hd).T.astype(BF16)


def _inproj(x, mod, g, win, tabs, masks, gq, gkv, wuq, wukv, *, batch, seq):
    t, d = x.shape
    tm = min(PROJ_TM, seq)
    per_seq = seq // tm
    row = lambda i: (i, 0)
    tab = lambda i: (i % per_seq, 0)
    head_rows = lambda i: (i // per_seq, 0, i % per_seq, 0)
    head_cols = lambda i: (i // per_seq, 0, 0, i % per_seq)
    hq = lambda w: pl.BlockSpec((None, DA_HEADS, tm, w), head_rows)
    vt = pl.BlockSpec((None, DA_HEADS, LANES, tm), head_cols)
    sd = jax.ShapeDtypeStruct
    return pl.pallas_call(
        _inproj_body,
        grid=(t // tm,),
        in_specs=[pl.BlockSpec((tm, d), row),
                  pl.BlockSpec((None, N_MOD, d), lambda i: (i // per_seq, 0, 0)),
                  _const_spec((1, d)),
                  _const_spec(win.shape),
                  pl.BlockSpec((tm, LANES), tab), pl.BlockSpec((tm, LANES), tab),
                  pl.BlockSpec((tm, LANES), tab), pl.BlockSpec((tm, LANES), tab),
                  _const_spec((1, LANES)), _const_spec((1, LANES)),
                  _const_spec((1, MLA_Q_RANK)), _const_spec((1, MLA_KV_RANK)),
                  _const_spec(wuq.shape), _const_spec(wukv.shape)],
        out_specs=[hq(LANES), hq(LANES), hq(LANES), vt, hq(MLA_QK_W), hq(MLA_QK_W), vt],
        out_shape=[sd((batch, DA_HEADS, seq, LANES), BF16), sd((batch, DA_HEADS, seq, LANES), BF16),
                   sd((batch, DA_HEADS, seq, LANES), BF16), sd((batch, DA_HEADS, LANES, seq), BF16),
                   sd((batch, MLA_HEADS, seq, MLA_QK_W), BF16), sd((batch, MLA_HEADS, seq, MLA_QK_W), BF16),
                   sd((batch, MLA_HEADS, LANES, seq), BF16)],
        compiler_params=_params("arbitrary"),
        name="inproj",
    )(x, mod, g, win, *tabs, *masks, gq, gkv, wuq, wukv)


class _Stream:
    def __init__(self, q_ref, q_mask, bias_lane, k_ref, vt_ref, acc_ref, sum_ref, qz_ref, k2max_ref, bufs, tk):
        self.q_ref, self.q_mask, self.bias_lane, self.k_ref, self.vt_ref = q_ref, q_mask, bias_lane, k_ref, vt_ref
        self.acc_ref, self.sum_ref, self.qz_ref, self.k2max_ref, self.bufs, self.tk = (
            acc_ref, sum_ref, qz_ref, k2max_ref, bufs, tk)

    def _keys(self, blk):
        return pl.ds(pl.multiple_of(blk * self.tk, self.tk), self.tk)

    def masked_q(self):
        q = self.q_ref[...].astype(F32)
        return q if self.q_mask is None else q * self.q_mask

    def store_key_norm(self, nblk):
        def blk(b, best):
            k = self.k_ref[self._keys(b), :].astype(F32)
            return jnp.maximum(best, jnp.max(jnp.sum(k * k, axis=1, keepdims=True), axis=0, keepdims=True))
        best = lax.fori_loop(0, nblk, blk, jnp.zeros((1, 1), F32))
        self.k2max_ref[...] = jnp.broadcast_to(best, self.k2max_ref.shape)

    def score_bound(self):
        q = self.masked_q()
        q2 = jnp.max(jnp.sum(q * q, axis=1, keepdims=True), axis=0, keepdims=True)
        return jnp.sqrt(q2 * self.k2max_ref[0:1, 0:1]) * 1.01 + 0.01

    def set_query(self, shift):
        q = self.masked_q()
        if shift is not None:
            lane = lax.broadcasted_iota(jnp.int32, q.shape, 1)
            q = jnp.where(lane == self.bias_lane, -shift, q)
        self.qz_ref[...] = q.astype(BF16)

    def _qk(self, blk):
        return lax.dot_general(self.k_ref[self._keys(blk), :], self.qz_ref[...], (((1,), (1,)), ((), ())),
                               preferred_element_type=F32)

    def _pv(self, blk, p):
        return jnp.dot(self.vt_ref[:, self._keys(blk)], p.astype(BF16), preferred_element_type=F32)

    def scores(self, blk, slot):
        s = self._qk(blk)
        self.bufs[slot][...] = s
        return jnp.max(s, axis=0, keepdims=True)

    def consume(self, blk, slot, blk_max, m, l):
        m_new = jnp.maximum(m, blk_max)
        p = jnp.exp2(self.bufs[slot][...] - m_new)
        alpha = jnp.exp2(m - m_new)
        l_new = alpha * l + jnp.sum(p, axis=0, keepdims=True)
        self.acc_ref[...] = alpha * self.acc_ref[...] + self._pv(blk, p)
        return m_new, l_new

    def accumulate(self, blk, part):
        p = jnp.exp2(self._qk(blk))
        self.acc_ref[...] += self._pv(blk, p)
        return part + jnp.sum(p.reshape(self.tk // 8, 8, p.shape[1]), axis=0)


def _flash_exact(streams, nblk, tq):
    m0 = jnp.full((1, tq), NEG_BIG, F32)
    l0 = jnp.zeros((1, tq), F32)
    for st in streams:
        st.set_query(None)
    first = tuple((m0, l0, st.scores(0, 0)) for st in streams)

    def pair(j, carry):
        b0 = 2 * j
        b2 = jnp.minimum(b0 + 2, nblk - 1)
        out = []
        for st, (m, l, max0) in zip(streams, carry):
            max1 = st.scores(b0 + 1, 1)
            m, l = st.consume(b0, 0, max0, m, l)
            max2 = st.scores(b2, 0)
            m, l = st.consume(b0 + 1, 1, max1, m, l)
            out.append((m, l, max2))
        return tuple(out)

    final = lax.fori_loop(0, nblk // 2, pair, first)
    for st, (_, l, _) in zip(streams, final):
        st.sum_ref[...] = l


def _flash_shifted(streams, bounds, nblk, tq):
    for st, bound in zip(streams, bounds):
        st.set_query(bound)

    unroll = SHIFT_UNROLL if nblk % SHIFT_UNROLL == 0 else 2

    def group(j, parts):
        for u in range(unroll):
            parts = tuple(st.accumulate(unroll * j + u, part) for st, part in zip(streams, parts))
        return parts

    parts = lax.fori_loop(0, nblk // unroll, group, tuple(jnp.zeros((8, tq), F32) for _ in streams))
    for st, part in zip(streams, parts):
        st.sum_ref[...] = jnp.sum(part, axis=0, keepdims=True)


def _flash(streams, nblk, tq, new_keys):
    assert nblk % 2 == 0

    @pl.when(new_keys)
    def _():
        for st in {id(st.k2max_ref): st for st in streams}.values():
            st.store_key_norm(nblk)

    for st in streams:
        st.acc_ref[...] = jnp.zeros(st.acc_ref.shape, F32)
    bounds = [st.score_bound() for st in streams]
    shift_ok = functools.reduce(jnp.maximum, bounds)[0, 0] <= SHIFT_MAX_BOUND

    @pl.when(shift_ok)
    def _():
        _flash_shifted(streams, bounds, nblk, tq)

    @pl.when(jnp.logical_not(shift_ok))
    def _():
        _flash_exact(streams, nblk, tq)


def _da_body(lam_ref, g_ref, m1_ref, m2_ref, q_ref, k1_ref, k2_ref, vt_ref, o_ref,
             acc_ref, sum_ref, qz_ref, k2max_ref, *bufs, tk, lambda_init):
    tq = q_ref.shape[0] // ATT_QSUB
    masks = (m1_ref[...], m2_ref[...])
    lam_scr = bufs[-1]
    new_head = pl.program_id(2) == 0

    @pl.when(new_head)
    def _():
        lq1, lk1, lq2, lk2 = (lam_ref[r:r + 1, :] for r in range(4))
        lam = (jnp.exp(jnp.sum(lq1 * lk1, axis=-1, keepdims=True))
               - jnp.exp(jnp.sum(lq2 * lk2, axis=-1, keepdims=True)) + lambda_init)
        lam_scr[...] = jnp.broadcast_to(lam, lam_scr.shape)

    def query_block(sub, carry):
        rows = pl.ds(pl.multiple_of(sub * tq, tq), tq)
        streams = [_Stream(q_ref.at[rows, :], masks[j], DA_BIAS_LANES[j], (k1_ref, k2_ref)[j], vt_ref, acc_ref.at[j],
                           sum_ref.at[j], qz_ref.at[j], k2max_ref.at[j], bufs[2 * j:2 * j + 2], tk) for j in range(2)]
        _flash(streams, k1_ref.shape[0] // tk, tq, jnp.logical_and(new_head, sub == 0))
        lam = lam_scr[0:1, 0:1]
        ot = acc_ref[0] / sum_ref[0] - lam * (acc_ref[1] / sum_ref[1])
        ot = ot * lax.rsqrt(jnp.mean(ot * ot, axis=0, keepdims=True) + EPS)
        o_ref[rows, :] = (ot.T * g_ref[...] * (1.0 - lambda_init)).astype(o_ref.dtype)
        return carry

    lax.fori_loop(0, ATT_QSUB, query_block, 0)


def _mla_body(q_ref, k_ref, vt_ref, o_ref, acc_ref, sum_ref, qz_ref, k2max_ref, *bufs, tk):
    nq = acc_ref.shape[0]
    tq = q_ref.shape[0] // (nq * ATT_QSUB)
    key_norm = k2max_ref.at[0]
    new_head = pl.program_id(2) == 0

    def query_blocks(sub, carry):
        rows = [pl.ds(pl.multiple_of((sub * nq + i) * tq, tq), tq) for i in range(nq)]
        streams = [_Stream(q_ref.at[rows[i], :], None, MLA_BIAS_LANE, k_ref, vt_ref, acc_ref.at[i], sum_ref.at[i],
                           qz_ref.at[i], key_norm, bufs[2 * i:2 * i + 2], tk) for i in range(nq)]
        _flash(streams, k_ref.shape[0] // tk, tq, jnp.logical_and(new_head, sub == 0))
        for i in range(nq):
            o_ref[rows[i], :] = (acc_ref[i] / sum_ref[i]).T.astype(o_ref.dtype)
        return carry

    lax.fori_loop(0, ATT_QSUB, query_blocks, 0)


def _attn_specs(batch, heads, seq, qk_w, nq=1):
    tq = min(ATT_TQ, seq // nq)
    tk = min(ATT_TK, seq)
    q_spec = pl.BlockSpec((None, None, nq * tq, qk_w), lambda b, h, i: (b, h, i, 0))
    k_spec = pl.BlockSpec((None, None, seq, qk_w), lambda b, h, i: (b, h, 0, 0))
    vt_spec = pl.BlockSpec((None, None, LANES, seq), lambda b, h, i: (b, h, 0, 0))
    o_spec = pl.BlockSpec((None, nq * tq, LANES), lambda b, h, i: (b, i, h))
    o_shape = jax.ShapeDtypeStruct((batch, seq, heads * LANES), BF16)
    return tq, tk, (batch, heads, seq // (nq * tq)), q_spec, k_spec, vt_spec, o_spec, o_shape


def _attn_scratch(streams, key_sets, tq, tk, qk_w):
    return ([pltpu.VMEM((streams, LANES, tq), F32),
             pltpu.VMEM((streams, 1, tq), F32),
             pltpu.VMEM((streams, tq, qk_w), BF16),
             pltpu.VMEM((key_sets, 8, LANES), F32)]
            + [pltpu.VMEM((tk, tq), F32)] * (2 * streams))


def _da_attn(lam_vecs, subln_g, masks, q, k1, k2, vt, *, lambda_init):
    batch, heads, seq, _ = q.shape
    tq, tk, grid, q_spec, k_spec, vt_spec, o_spec, o_shape = _attn_specs(batch, heads, seq, LANES, ATT_QSUB)
    lane_row = pl.BlockSpec((1, LANES), lambda b, h, i: (0, 0))
    return pl.pallas_call(
        functools.partial(_da_body, tk=tk, lambda_init=lambda_init),
        grid=grid,
        in_specs=[pl.BlockSpec(lam_vecs.shape, lambda b, h, i: (0, 0)), lane_row, lane_row, lane_row,
                  q_spec, k_spec, k_spec, vt_spec],
        out_specs=o_spec,
        out_shape=o_shape,
        scratch_shapes=_attn_scratch(2, 2, tq, tk, LANES) + [pltpu.VMEM((8, LANES), F32)],
        compiler_params=_params("arbitrary", "arbitrary", "arbitrary"),
        name="da_attn",
    )(lam_vecs, subln_g, *masks, q, k1, k2, vt)


def _mla_attn(q, k, vt):
    batch, heads, seq, qk_w = q.shape
    nq = MLA_QSTREAMS
    tq, tk, grid, q_spec, k_spec, vt_spec, o_spec, o_shape = _attn_specs(batch, heads, seq, qk_w, nq * ATT_QSUB)
    return pl.pallas_call(
        functools.partial(_mla_body, tk=tk),
        grid=grid,
        in_specs=[q_spec, k_spec, vt_spec],
        out_specs=o_spec,
        out_shape=o_shape,
        scratch_shapes=_attn_scratch(nq, 1, tq, tk, qk_w),
        compiler_params=_params("arbitrary", "arbitrary", "arbitrary"),
        name="mla_attn",
    )(q, k, vt)


def _outproj_body(x_ref, mod_ref, oda_ref, oml_ref, wo_ref, o_ref):
    split = oda_ref.shape[1]
    mix = (jnp.dot(oda_ref[...], wo_ref[0:split, :], preferred_element_type=F32)
           + jnp.dot(oml_ref[...], wo_ref[split:, :], preferred_element_type=F32))
    o_ref[...] = x_ref[...] + mod_ref[5:6, :] * mix


def _outproj(x, mod, o_da, o_mla, wo, *, seq):
    t, d = x.shape
    tm = min(OUT_TM, seq)
    per_seq = seq // tm
    row = lambda i: (i, 0)
    return pl.pallas_call(
        _outproj_body,
        grid=(t // tm,),
        in_specs=[pl.BlockSpec((tm, d), row),
                  pl.BlockSpec((None, N_MOD, d), lambda i: (i // per_seq, 0, 0)),
                  pl.BlockSpec((tm, o_da.shape[1]), row),
                  pl.BlockSpec((tm, o_mla.shape[1]), row),
                  _const_spec(wo.shape)],
        out_specs=pl.BlockSpec((tm, d), row),
        out_shape=jax.ShapeDtypeStruct((t, d), F32),
        compiler_params=_params("arbitrary"),
        name="outproj",
    )(x, mod, o_da, o_mla, wo)


def _da_lane_layout():
    src = np.zeros(LANES, np.int32)
    rot = np.full(LANES, -1, np.int32)
    sign = np.zeros(LANES, np.float32)
    mask = np.zeros((2, LANES), np.float32)
    half = DA_ROT // 2
    for j in range(2):
        base = j * DA_QK_DIM
        for i in range(half):
            src[j * half + i] = base + i
            rot[j * half + i] = i
            sign[j * half + i] = -1.0
            src[64 + j * half + i] = base + half + i
            rot[64 + j * half + i] = i
            sign[64 + j * half + i] = 1.0
            mask[j, j * half + i] = mask[j, 64 + j * half + i] = 1.0
        plain = DA_QK_DIM - DA_ROT
        start = DA_ROT + j * 64
        for i in range(plain):
            src[start + i] = base + DA_ROT + i
            mask[j, start + i] = 1.0
    return src, rot, sign, mask


def _mla_lane_layout():
    src = np.full(LANES, -1, np.int32)
    rot = np.full(LANES, -1, np.int32)
    sign = np.zeros(LANES, np.float32)
    half = MLA_ROPE // 2
    for i in range(half):
        src[i], rot[i], sign[i] = i, i, -1.0
        src[64 + i], rot[64 + i], sign[64 + i] = half + i, i, 1.0
    return src, rot, sign


def _take_cols(w, src):
    src = [int(v) for v in src]
    parts, start = [], 0
    for i in range(1, len(src) + 1):
        same_run = i < len(src) and ((src[i] < 0 and src[i - 1] < 0) or (src[i - 1] >= 0 and src[i] == src[i - 1] + 1))
        if not same_run:
            width = i - start
            parts.append(jnp.zeros((w.shape[0], width), w.dtype) if src[start] < 0
                         else w[:, src[start]:src[start] + width])
            start = i
    return jnp.concatenate(parts, axis=1)


def _rope_tables(seq, dim, rot, sign):
    inv = ROPE_THETA ** (-jnp.arange(0, dim, 2, dtype=F32) / dim)
    ang = jnp.arange(seq, dtype=F32)[:, None] * _take_cols(inv[None, :], rot)
    return jnp.cos(ang), jnp.sin(ang) * jnp.asarray(sign)


def _prep_layer(l, ffn1_w1, ffn1_w3, ffn1_w2, w_in, mla_w_uq, mla_w_ukv, w_o, ffn2_w1, ffn2_w3, ffn2_w2):
    da_src, _, _, _ = _da_lane_layout()
    ml_src, _, _ = _mla_lane_layout()
    head_src = np.concatenate([h * 2 * DA_QK_DIM + da_src for h in range(DA_HEADS)])
    o1, o2, o3 = DA_Q_W, 2 * DA_Q_W, 2 * DA_Q_W + DA_V_W
    o5 = o3 + MLA_Q_RANK + MLA_KV_RANK
    wi = w_in[l].astype(BF16)
    win = jnp.concatenate([_take_cols(wi[:, :o1], head_src),
                           _take_cols(wi[:, o1:o2], head_src),
                           wi[:, o2:o5],
                           _take_cols(wi[:, o5:], ml_src)], axis=1)
    per_q = MLA_NOPE + MLA_ROPE
    uq_src = np.concatenate([np.concatenate([h * per_q + np.arange(MLA_NOPE),
                                             np.where(ml_src >= 0, h * per_q + MLA_NOPE + ml_src, -1)])
                             for h in range(MLA_HEADS)])
    per_kv = MLA_NOPE + MLA_V
    ukv_src = np.concatenate([h * per_kv + np.arange(MLA_NOPE) for h in range(MLA_HEADS)]
                             + [h * per_kv + MLA_NOPE + np.arange(MLA_V) for h in range(MLA_HEADS)])
    return dict(
        f1=(ffn1_w1[l].astype(BF16), ffn1_w3[l].astype(BF16), ffn1_w2[l].astype(BF16)),
        f2=(ffn2_w1[l].astype(BF16), ffn2_w3[l].astype(BF16), ffn2_w2[l].astype(BF16)),
        win=win,
        wuq=_take_cols(mla_w_uq[l].astype(BF16), uq_src),
        wukv=_take_cols(mla_w_ukv[l].astype(BF16), ukv_src),
        wo=w_o[l].astype(BF16),
    )


def _encode_group(x, mods, prepped, small, final_norm_g, tabs):
    batch, seq, d = x.shape
    xt = x.reshape(batch * seq, d)
    da_mask = _da_lane_layout()[3]
    masks = (jnp.asarray(da_mask[0:1]), jnp.asarray(da_mask[1:2]))
    depth = len(prepped)
    row = lambda v: v.reshape(1, -1)
    for l in range(depth):
        p, s, mod = prepped[l], small[l], mods[l]
        lambda_init = 0.8 - 0.6 * math.exp(-0.3 * l)
        xt = _ffn(xt, mod, row(s["ffn1_norm"]), *p["f1"], row(s["ffn1_norm"]),
                  seq=seq, mod_base=0, final_norm=False)
        qda, k1, k2, vtda, qml, kml, vtml = _inproj(
            xt, mod, row(s["attn_norm"]), p["win"], tabs, masks, row(s["mla_q_norm"]), row(s["mla_kv_norm"]),
            p["wuq"], p["wukv"], batch=batch, seq=seq)
        o_da = _da_attn(s["lam_vecs"], row(s["da_subln"]), masks, qda, k1, k2, vtda, lambda_init=lambda_init)
        o_mla = _mla_attn(qml, kml, vtml)
        xt = _outproj(xt, mod, o_da.reshape(batch * seq, -1), o_mla.reshape(batch * seq, -1),
                      p["wo"], seq=seq)
        last = l == depth - 1
        xt = _ffn(xt, mod, row(s["ffn2_norm"]), *p["f2"], row(final_norm_g),
                  seq=seq, mod_base=6, final_norm=last)
    return xt.reshape(batch, seq, d)


def kernel(x_prompt, x_sample, c_prompt, c_sample, ffn1_norm, ffn1_w1, ffn1_w3, ffn1_w2, attn_norm, w_in,
           da_lambda_q1, da_lambda_k1, da_lambda_q2, da_lambda_k2, da_subln, mla_q_norm, mla_w_uq, mla_kv_norm,
           mla_w_ukv, w_o, ffn2_norm, ffn2_w1, ffn2_w3, ffn2_w2, w_ada, b_ada, final_norm):
    depth = w_in.shape[0]
    nb_p, nb_s = c_prompt.shape[0], c_sample.shape[0]
    c_all = jnp.concatenate([c_prompt, c_sample], axis=0)
    rows = -(-c_all.shape[0] // 8) * 8
    c_pad = jnp.pad(c_all, ((0, rows - c_all.shape[0]), (0, 0)))

    prepped, small, mods_p, mods_s = [], [], [], []
    for l in range(depth):
        prepped.append(_prep_layer(l, ffn1_w1, ffn1_w3, ffn1_w2, w_in, mla_w_uq, mla_w_ukv, w_o,
                                   ffn2_w1, ffn2_w3, ffn2_w2))
        small.append(dict(
            ffn1_norm=ffn1_norm[l], attn_norm=attn_norm[l], ffn2_norm=ffn2_norm[l], da_subln=da_subln[l],
            mla_q_norm=mla_q_norm[l], mla_kv_norm=mla_kv_norm[l],
            lam_vecs=jnp.stack([da_lambda_q1[l], da_lambda_k1[l], da_lambda_q2[l], da_lambda_k2[l]])))
        m = _ada(c_pad, w_ada[l], b_ada[l].reshape(1, -1))
        mods_p.append(m[:nb_p].reshape(nb_p, N_MOD, D_MODEL))
        mods_s.append(m[nb_p:nb_p + nb_s].reshape(nb_s, N_MOD, D_MODEL))

    _, da_rot, da_sign, _ = _da_lane_layout()
    _, ml_rot, ml_sign = _mla_lane_layout()
    max_seq = max(x_prompt.shape[1], x_sample.shape[1])
    tabs = _rope_tables(max_seq, DA_ROT, da_rot, da_sign) + _rope_tables(max_seq, MLA_ROPE, ml_rot, ml_sign)

    y_prompt = _encode_group(x_prompt, mods_p, prepped, small, final_norm, tabs)
    y_sample = _encode_group(x_sample, mods_s, prepped, small, final_norm, tabs)
    return (y_prompt, y_sample)
```

```python
import functools
import math

import jax
import jax.numpy as jnp
import numpy as np
from jax import lax
from jax.experimental import pallas as pl
from jax.experimental.pallas import tpu as pltpu

D_MODEL = 2048
D_FF = 5632
N_MOD = 9
DA_HEADS = 8
DA_QK_DIM = 64
DA_V_DIM = 128
DA_ROT = 16
DA_Q_W = DA_HEADS * 2 * DA_QK_DIM
DA_V_W = DA_HEADS * DA_V_DIM
MLA_HEADS = 8
MLA_NOPE = 128
MLA_ROPE = 64
MLA_V = 128
MLA_Q_RANK = 512
MLA_KV_RANK = 256
ROPE_THETA = 500000.0
EPS = 1e-6
LANES = 128
MLA_QK_W = 2 * LANES
VMEM_LIMIT = 56 * 1024 * 1024
FFN_VMEM_LIMIT = 60 * 1024 * 1024
NEG_BIG = -1e30
LOG2E = math.log2(math.e)
SHIFT_MAX_BOUND = 40.0
DA_BIAS_LANES = (8, 0)
MLA_BIAS_LANE = LANES + 32

FFN_TM, FFN_TF = 1024, 512
FFN_ROW_CHUNK = 256
PROJ_TM = 256
OUT_TM = 512
DA_TQ, DA_TK = 1024, 1024
MLA_TQ, MLA_TK = 1024, 2048
MLA_QSTREAMS = 1
ATT_QSUB = 2
SHIFT_UNROLL = 4
ADA_TN = 1024

BF16 = jnp.bfloat16
F32 = jnp.float32


def _params(*sem, vmem=VMEM_LIMIT):
    return pltpu.CompilerParams(dimension_semantics=sem, vmem_limit_bytes=vmem)


def _rms(x, g):
    return x * lax.rsqrt(jnp.mean(x * x, axis=-1, keepdims=True) + EPS) * g


def _silu(x):
    return x / (1.0 + jnp.exp(-x))


def _const_spec(shape):
    zeros = (0,) * len(shape)
    return pl.BlockSpec(shape, lambda *_: zeros, pipeline_mode=pl.Buffered(1))


def _ada_body(c_ref, w_ref, b_ref, o_ref):
    a = _silu(c_ref[...]).astype(BF16)
    o_ref[...] = jnp.dot(a, w_ref[...].astype(BF16), preferred_element_type=F32) + b_ref[...]


def _ada(c_pad, w, b):
    rows, d = c_pad.shape
    n = w.shape[1]
    return pl.pallas_call(
        _ada_body,
        grid=(n // ADA_TN,),
        in_specs=[pl.BlockSpec((rows, d), lambda j: (0, 0)),
                  pl.BlockSpec((d, ADA_TN), lambda j: (0, j)),
                  pl.BlockSpec((1, ADA_TN), lambda j: (0, j))],
        out_specs=pl.BlockSpec((rows, ADA_TN), lambda j: (0, j)),
        out_shape=jax.ShapeDtypeStruct((rows, n), F32),
        compiler_params=_params("arbitrary"),
        name="ada",
    )(c_pad, w, b)


def _ffn_body(x_ref, mod_ref, g_ref, w1_ref, w3_ref, w2_ref, fg_ref, o_ref, h_ref, *, mod_base, final_norm):
    j = pl.program_id(1)
    last = pl.num_programs(1) - 1
    chunks = [pl.ds(r, FFN_ROW_CHUNK) for r in range(0, x_ref.shape[0], FFN_ROW_CHUNK)]

    def swiglu(h):
        a = jnp.dot(h, w1_ref[...], preferred_element_type=F32)
        b = jnp.dot(h, w3_ref[...], preferred_element_type=F32)
        return jnp.dot((_silu(a) * b).astype(BF16), w2_ref[...], preferred_element_type=F32)

    @pl.when(j == 0)
    def _():
        shift = mod_ref[mod_base:mod_base + 1, :]
        scale = mod_ref[mod_base + 1:mod_base + 2, :]
        for rows in chunks:
            h = (_rms(x_ref[rows, :], g_ref[...]) * (1.0 + scale) + shift).astype(BF16)
            h_ref[rows, :] = h
            o_ref[rows, :] = swiglu(h)

    @pl.when(jnp.logical_and(j > 0, j < last))
    def _():
        o_ref[...] += swiglu(h_ref[...])

    @pl.when(j == last)
    def _():
        gate = mod_ref[mod_base + 2:mod_base + 3, :]
        for rows in chunks:
            y = x_ref[rows, :] + 0.5 * gate * (o_ref[rows, :] + swiglu(h_ref[rows, :]))
            if final_norm:
                y = _rms(y, fg_ref[...])
            o_ref[rows, :] = y


def _ffn(x, mod, g, w1, w3, w2, fg, *, seq, mod_base, final_norm):
    t, d = x.shape
    f = w1.shape[1]
    tm = min(FFN_TM, seq)
    tf = FFN_TF
    assert f // tf >= 2, "the first and the last hidden-dim step must be different steps"
    per_seq = seq // tm
    return pl.pallas_call(
        functools.partial(_ffn_body, mod_base=mod_base, final_norm=final_norm),
        grid=(t // tm, f // tf),
        in_specs=[pl.BlockSpec((tm, d), lambda i, j: (i, 0)),
                  pl.BlockSpec((None, N_MOD, d), lambda i, j: (i // per_seq, 0, 0)),
                  pl.BlockSpec((1, d), lambda i, j: (0, 0)),
                  pl.BlockSpec((d, tf), lambda i, j: (0, j)),
                  pl.BlockSpec((d, tf), lambda i, j: (0, j)),
                  pl.BlockSpec((tf, d), lambda i, j: (j, 0)),
                  pl.BlockSpec((1, d), lambda i, j: (0, 0))],
        out_specs=pl.BlockSpec((tm, d), lambda i, j: (i, 0)),
        out_shape=jax.ShapeDtypeStruct((t, d), F32),
        scratch_shapes=[pltpu.VMEM((tm, d), BF16)],
        compiler_params=_params("arbitrary", "arbitrary", vmem=FFN_VMEM_LIMIT),
        name="ffn_final" if final_norm else "ffn",
    )(x, mod, g, w1, w3, w2, fg)


def _rope(t, cos, sin_signed):
    return t * cos + pltpu.roll(t, LANES // 2, axis=1) * sin_signed


def _inproj_body(x_ref, mod_ref, g_ref, win_ref, cda_ref, sda_ref, cml_ref, sml_ref, m1_ref, m2_ref,
                 gq_ref, gkv_ref, wuq_ref, wukv_ref,
                 qda_ref, k1_ref, k2_ref, vtda_ref, qml_ref, kml_ref, vtml_ref):
    shift = mod_ref[3:4, :]
    scale = mod_ref[4:5, :]
    h = (_rms(x_ref[...], g_ref[...]) * (1.0 + scale) + shift).astype(BF16)
    cda, sda = cda_ref[...], sda_ref[...]
    cml, sml = cml_ref[...], sml_ref[...]
    mask1, mask2 = m1_ref[...], m2_ref[...]
    da_scale = DA_QK_DIM ** -0.5 * LOG2E
    mla_scale = (MLA_NOPE + MLA_ROPE) ** -0.5 * LOG2E

    def tile(a, k):
        return a[:, k * LANES:(k + 1) * LANES]

    def one_hot(lane):
        return (lax.broadcasted_iota(jnp.int32, (1, LANES), 1) == lane).astype(F32)

    pq = jnp.dot(h, win_ref[:, 0:DA_Q_W], preferred_element_type=F32)
    for hd in range(DA_HEADS):
        qda_ref[hd] = (_rope(tile(pq, hd), cda, sda) * da_scale).astype(BF16)
    pk = jnp.dot(h, win_ref[:, DA_Q_W:2 * DA_Q_W], preferred_element_type=F32)
    for hd in range(DA_HEADS):
        r = _rope(tile(pk, hd), cda, sda)
        k1_ref[hd] = (r * mask1 + one_hot(DA_BIAS_LANES[0])).astype(BF16)
        k2_ref[hd] = (r * mask2 + one_hot(DA_BIAS_LANES[1])).astype(BF16)
    pv = jnp.dot(h, win_ref[:, 2 * DA_Q_W:2 * DA_Q_W + DA_V_W], preferred_element_type=F32)
    for hd in range(DA_HEADS):
        vtda_ref[hd] = tile(pv, hd).T.astype(BF16)

    o3 = 2 * DA_Q_W + DA_V_W
    pr = jnp.dot(h, win_ref[:, o3:], preferred_element_type=F32)
    cq = _rms(pr[:, 0:MLA_Q_RANK], gq_ref[...]).astype(BF16)
    ckv = _rms(pr[:, MLA_Q_RANK:MLA_Q_RANK + MLA_KV_RANK], gkv_ref[...]).astype(BF16)
    kpe = (_rope(pr[:, MLA_Q_RANK + MLA_KV_RANK:], cml, sml) + one_hot(MLA_BIAS_LANE - LANES)).astype(BF16)

    qm = jnp.dot(cq, wuq_ref[...], preferred_element_type=F32)
    for hd in range(MLA_HEADS):
        qml_ref[hd, :, 0:LANES] = (tile(qm, 2 * hd) * mla_scale).astype(BF16)
        qml_ref[hd, :, LANES:] = (_rope(tile(qm, 2 * hd + 1), cml, sml) * mla_scale).astype(BF16)
    kv = jnp.dot(ckv, wukv_ref[...], preferred_element_type=F32)
    for hd in range(MLA_HEADS):
        kml_ref[hd, :, 0:LANES] = tile(kv, hd).astype(BF16)
        kml_ref[hd, :, LANES:] = kpe
        vtml_ref[hd] = tile(kv, MLA_HEADS + hd).T.astype(BF16)


def _inproj(x, mod, g, win, tabs, masks, gq, gkv, wuq, wukv, *, batch, seq):
    t, d = x.shape
    tm = min(PROJ_TM, seq)
    per_seq = seq // tm
    row = lambda i: (i, 0)
    tab = lambda i: (i % per_seq, 0)
    head_rows = lambda i: (i // per_seq, 0, i % per_seq, 0)
    head_cols = lambda i: (i // per_seq, 0, 0, i % per_seq)
    hq = lambda w: pl.BlockSpec((None, DA_HEADS, tm, w), head_rows)
    vt = pl.BlockSpec((None, DA_HEADS, LANES, tm), head_cols)
    sd = jax.ShapeDtypeStruct
    return pl.pallas_call(
        _inproj_body,
        grid=(t // tm,),
        in_specs=[pl.BlockSpec((tm, d), row),
                  pl.BlockSpec((None, N_MOD, d), lambda i: (i // per_seq, 0, 0)),
                  _const_spec((1, d)),
                  _const_spec(win.shape),
                  pl.BlockSpec((tm, LANES), tab), pl.BlockSpec((tm, LANES), tab),
                  pl.BlockSpec((tm, LANES), tab), pl.BlockSpec((tm, LANES), tab),
                  _const_spec((1, LANES)), _const_spec((1, LANES)),
                  _const_spec((1, MLA_Q_RANK)), _const_spec((1, MLA_KV_RANK)),
                  _const_spec(wuq.shape), _const_spec(wukv.shape)],
        out_specs=[hq(LANES), hq(LANES), hq(LANES), vt, hq(MLA_QK_W), hq(MLA_QK_W), vt],
        out_shape=[sd((batch, DA_HEADS, seq, LANES), BF16), sd((batch, DA_HEADS, seq, LANES), BF16),
                   sd((batch, DA_HEADS, seq, LANES), BF16), sd((batch, DA_HEADS, LANES, seq), BF16),
                   sd((batch, MLA_HEADS, seq, MLA_QK_W), BF16), sd((batch, MLA_HEADS, seq, MLA_QK_W), BF16),
                   sd((batch, MLA_HEADS, LANES, seq), BF16)],
        compiler_params=_params("arbitrary"),
        name="inproj",
    )(x, mod, g, win, *tabs, *masks, gq, gkv, wuq, wukv)


class _Stream:
    def __init__(self, q_ref, q_mask, bias_lane, k_ref, vt_ref, acc_ref, sum_ref, qz_ref, k2max_ref, bufs, tk):
        self.q_ref, self.q_mask, self.bias_lane, self.k_ref, self.vt_ref = q_ref, q_mask, bias_lane, k_ref, vt_ref
        self.acc_ref, self.sum_ref, self.qz_ref, self.k2max_ref, self.bufs, self.tk = (
            acc_ref, sum_ref, qz_ref, k2max_ref, bufs, tk)

    def _keys(self, blk):
        return pl.ds(pl.multiple_of(blk * self.tk, self.tk), self.tk)

    def masked_q(self):
        q = self.q_ref[...].astype(F32)
        return q if self.q_mask is None else q * self.q_mask

    def store_key_norm(self, nblk):
        def blk(b, best):
            k = self.k_ref[self._keys(b), :].astype(F32)
            return jnp.maximum(best, jnp.max(jnp.sum(k * k, axis=1, keepdims=True), axis=0, keepdims=True))
        best = lax.fori_loop(0, nblk, blk, jnp.zeros((1, 1), F32))
        self.k2max_ref[...] = jnp.broadcast_to(best, self.k2max_ref.shape)

    def score_bound(self):
        q = self.masked_q()
        q2 = jnp.max(jnp.sum(q * q, axis=1, keepdims=True), axis=0, keepdims=True)
        return jnp.sqrt(q2 * self.k2max_ref[0:1, 0:1]) * 1.01 + 0.01

    def set_query(self, shift):
        q = self.masked_q()
        if shift is not None:
            lane = lax.broadcasted_iota(jnp.int32, q.shape, 1)
            q = jnp.where(lane == self.bias_lane, -shift, q)
        self.qz_ref[...] = q.astype(BF16)

    def _qk(self, blk):
        return lax.dot_general(self.k_ref[self._keys(blk), :], self.qz_ref[...], (((1,), (1,)), ((), ())),
                               preferred_element_type=F32)

    def _pv(self, blk, p):
        return jnp.dot(self.vt_ref[:, self._keys(blk)], p.astype(BF16), preferred_element_type=F32)

    def scores(self, blk, slot):
        s = self._qk(blk)
        self.bufs[slot][...] = s
        return jnp.max(s, axis=0, keepdims=True)

    def consume(self, blk, slot, blk_max, m, l):
        m_new = jnp.maximum(m, blk_max)
        p = jnp.exp2(self.bufs[slot][...] - m_new)
        alpha = jnp.exp2(m - m_new)
        l_new = alpha * l + jnp.sum(p, axis=0, keepdims=True)
        self.acc_ref[...] = alpha * self.acc_ref[...] + self._pv(blk, p)
        return m_new, l_new

    def accumulate(self, blk, part):
        p = jnp.exp2(self._qk(blk))
        self.acc_ref[...] += self._pv(blk, p)
        return part + jnp.sum(p.reshape(self.tk // 8, 8, p.shape[1]), axis=0)


def _flash_exact(streams, nblk, tq):
    m0 = jnp.full((1, tq), NEG_BIG, F32)
    l0 = jnp.zeros((1, tq), F32)
    for st in streams:
        st.set_query(None)
    first = tuple((m0, l0, st.scores(0, 0)) for st in streams)

    def pair(j, carry):
        b0 = 2 * j
        b2 = jnp.minimum(b0 + 2, nblk - 1)
        out = []
        for st, (m, l, max0) in zip(streams, carry):
            max1 = st.scores(b0 + 1, 1)
            m, l = st.consume(b0, 0, max0, m, l)
            max2 = st.scores(b2, 0)
            m, l = st.consume(b0 + 1, 1, max1, m, l)
            out.append((m, l, max2))
        return tuple(out)

    final = lax.fori_loop(0, nblk // 2, pair, first)
    for st, (_, l, _) in zip(streams, final):
        st.sum_ref[...] = l


def _flash_shifted(streams, bounds, nblk, tq):
    for st, bound in zip(streams, bounds):
        st.set_query(bound)

    unroll = SHIFT_UNROLL if nblk % SHIFT_UNROLL == 0 else 2

    def group(j, parts):
        for u in range(unroll):
            parts = tuple(st.accumulate(unroll * j + u, part) for st, part in zip(streams, parts))
        return parts

    parts = lax.fori_loop(0, nblk // unroll, group, tuple(jnp.zeros((8, tq), F32) for _ in streams))
    for st, part in zip(streams, parts):
        st.sum_ref[...] = jnp.sum(part, axis=0, keepdims=True)


def _flash(streams, nblk, tq, new_keys):
    assert nblk % 2 == 0

    @pl.when(new_keys)
    def _():
        for st in {id(st.k2max_ref): st for st in streams}.values():
            st.store_key_norm(nblk)

    for st in streams:
        st.acc_ref[...] = jnp.zeros(st.acc_ref.shape, F32)
    bounds = [st.score_bound() for st in streams]
    shift_ok = functools.reduce(jnp.maximum, bounds)[0, 0] <= SHIFT_MAX_BOUND

    @pl.when(shift_ok)
    def _():
        _flash_shifted(streams, bounds, nblk, tq)

    @pl.when(jnp.logical_not(shift_ok))
    def _():
        _flash_exact(streams, nblk, tq)


def _da_body(lam_ref, g_ref, m1_ref, m2_ref, q_ref, k1_ref, k2_ref, vt_ref, o_ref,
             acc_ref, sum_ref, qz_ref, k2max_ref, *bufs, tk, lambda_init):
    tq = q_ref.shape[0] // ATT_QSUB
    masks = (m1_ref[...], m2_ref[...])
    lam_scr = bufs[-1]
    new_head = pl.program_id(2) == 0

    @pl.when(new_head)
    def _():
        lq1, lk1, lq2, lk2 = (lam_ref[r:r + 1, :] for r in range(4))
        lam = (jnp.exp(jnp.sum(lq1 * lk1, axis=-1, keepdims=True))
               - jnp.exp(jnp.sum(lq2 * lk2, axis=-1, keepdims=True)) + lambda_init)
        lam_scr[...] = jnp.broadcast_to(lam, lam_scr.shape)

    def query_block(sub, carry):
        rows = pl.ds(pl.multiple_of(sub * tq, tq), tq)
        streams = [_Stream(q_ref.at[rows, :], masks[j], DA_BIAS_LANES[j], (k1_ref, k2_ref)[j], vt_ref, acc_ref.at[j],
                           sum_ref.at[j], qz_ref.at[j], k2max_ref.at[j], bufs[2 * j:2 * j + 2], tk) for j in range(2)]
        _flash(streams, k1_ref.shape[0] // tk, tq, jnp.logical_and(new_head, sub == 0))
        lam = lam_scr[0:1, 0:1]
        ot = acc_ref[0] / sum_ref[0] - lam * (acc_ref[1] / sum_ref[1])
        ot = ot * lax.rsqrt(jnp.mean(ot * ot, axis=0, keepdims=True) + EPS)
        o_ref[rows, :] = (ot.T * g_ref[...] * (1.0 - lambda_init)).astype(o_ref.dtype)
        return carry

    lax.fori_loop(0, ATT_QSUB, query_block, 0)


def _mla_body(q_ref, k_ref, vt_ref, o_ref, acc_ref, sum_ref, qz_ref, k2max_ref, *bufs, tk):
    nq = acc_ref.shape[0]
    tq = q_ref.shape[0] // (nq * ATT_QSUB)
    key_norm = k2max_ref.at[0]
    new_head = pl.program_id(2) == 0

    def query_blocks(sub, carry):
        rows = [pl.ds(pl.multiple_of((sub * nq + i) * tq, tq), tq) for i in range(nq)]
        streams = [_Stream(q_ref.at[rows[i], :], None, MLA_BIAS_LANE, k_ref, vt_ref, acc_ref.at[i], sum_ref.at[i],
                           qz_ref.at[i], key_norm, bufs[2 * i:2 * i + 2], tk) for i in range(nq)]
        _flash(streams, k_ref.shape[0] // tk, tq, jnp.logical_and(new_head, sub == 0))
        for i in range(nq):
            o_ref[rows[i], :] = (acc_ref[i] / sum_ref[i]).T.astype(o_ref.dtype)
        return carry

    lax.fori_loop(0, ATT_QSUB, query_blocks, 0)


def _attn_specs(batch, heads, seq, qk_w, nq, tq_max, tk_max):
    tq = min(tq_max, seq // nq)
    tk = min(tk_max, seq)
    q_spec = pl.BlockSpec((None, None, nq * tq, qk_w), lambda b, h, i: (b, h, i, 0))
    k_spec = pl.BlockSpec((None, None, seq, qk_w), lambda b, h, i: (b, h, 0, 0))
    vt_spec = pl.BlockSpec((None, None, LANES, seq), lambda b, h, i: (b, h, 0, 0))
    o_spec = pl.BlockSpec((None, nq * tq, LANES), lambda b, h, i: (b, i, h))
    o_shape = jax.ShapeDtypeStruct((batch, seq, heads * LANES), BF16)
    return tq, tk, (batch, heads, seq // (nq * tq)), q_spec, k_spec, vt_spec, o_spec, o_shape


def _attn_scratch(streams, key_sets, tq, tk, qk_w):
    return ([pltpu.VMEM((streams, LANES, tq), F32),
             pltpu.VMEM((streams, 1, tq), F32),
             pltpu.VMEM((streams, tq, qk_w), BF16),
             pltpu.VMEM((key_sets, 8, LANES), F32)]
            + [pltpu.VMEM((tk, tq), F32)] * (2 * streams))


def _da_attn(lam_vecs, subln_g, masks, q, k1, k2, vt, *, lambda_init):
    batch, heads, seq, _ = q.shape
    tq, tk, grid, q_spec, k_spec, vt_spec, o_spec, o_shape = _attn_specs(
        batch, heads, seq, LANES, ATT_QSUB, DA_TQ, DA_TK)
    lane_row = pl.BlockSpec((1, LANES), lambda b, h, i: (0, 0))
    return pl.pallas_call(
        functools.partial(_da_body, tk=tk, lambda_init=lambda_init),
        grid=grid,
        in_specs=[pl.BlockSpec(lam_vecs.shape, lambda b, h, i: (0, 0)), lane_row, lane_row, lane_row,
                  q_spec, k_spec, k_spec, vt_spec],
        out_specs=o_spec,
        out_shape=o_shape,
        scratch_shapes=_attn_scratch(2, 2, tq, tk, LANES) + [pltpu.VMEM((8, LANES), F32)],
        compiler_params=_params("arbitrary", "arbitrary", "arbitrary"),
        name="da_attn",
    )(lam_vecs, subln_g, *masks, q, k1, k2, vt)


def _mla_attn(q, k, vt):
    batch, heads, seq, qk_w = q.shape
    nq = MLA_QSTREAMS
    tq, tk, grid, q_spec, k_spec, vt_spec, o_spec, o_shape = _attn_specs(
        batch, heads, seq, qk_w, nq * ATT_QSUB, MLA_TQ, MLA_TK)
    return pl.pallas_call(
        functools.partial(_mla_body, tk=tk),
        grid=grid,
        in_specs=[q_spec, k_spec, vt_spec],
        out_specs=o_spec,
        out_shape=o_shape,
        scratch_shapes=_attn_scratch(nq, 1, tq, tk, qk_w),
        compiler_params=_params("arbitrary", "arbitrary", "arbitrary"),
        name="mla_attn",
    )(q, k, vt)


def _outproj_body(x_ref, mod_ref, oda_ref, oml_ref, wo_ref, o_ref):
    split = oda_ref.shape[1]
    mix = (jnp.dot(oda_ref[...], wo_ref[0:split, :], preferred_element_type=F32)
           + jnp.dot(oml_ref[...], wo_ref[split:, :], preferred_element_type=F32))
    o_ref[...] = x_ref[...] + mod_ref[5:6, :] * mix


def _outproj(x, mod, o_da, o_mla, wo, *, seq):
    t, d = x.shape
    tm = min(OUT_TM, seq)
    per_seq = seq // tm
    row = lambda i: (i, 0)
    return pl.pallas_call(
        _outproj_body,
        grid=(t // tm,),
        in_specs=[pl.BlockSpec((tm, d), row),
                  pl.BlockSpec((None, N_MOD, d), lambda i: (i // per_seq, 0, 0)),
                  pl.BlockSpec((tm, o_da.shape[1]), row),
                  pl.BlockSpec((tm, o_mla.shape[1]), row),
                  _const_spec(wo.shape)],
        out_specs=pl.BlockSpec((tm, d), row),
        out_shape=jax.ShapeDtypeStruct((t, d), F32),
        compiler_params=_params("arbitrary"),
        name="outproj",
    )(x, mod, o_da, o_mla, wo)


def _da_lane_layout():
    src = np.zeros(LANES, np.int32)
    rot = np.full(LANES, -1, np.int32)
    sign = np.zeros(LANES, np.float32)
    mask = np.zeros((2, LANES), np.float32)
    half = DA_ROT // 2
    for j in range(2):
        base = j * DA_QK_DIM
        for i in range(half):
            src[j * half + i] = base + i
            rot[j * half + i] = i
            sign[j * half + i] = -1.0
            src[64 + j * half + i] = base + half + i
            rot[64 + j * half + i] = i
            sign[64 + j * half + i] = 1.0
            mask[j, j * half + i] = mask[j, 64 + j * half + i] = 1.0
        plain = DA_QK_DIM - DA_ROT
        start = DA_ROT + j * 64
        for i in range(plain):
            src[start + i] = base + DA_ROT + i
            mask[j, start + i] = 1.0
    return src, rot, sign, mask


def _mla_lane_layout():
    src = np.full(LANES, -1, np.int32)
    rot = np.full(LANES, -1, np.int32)
    sign = np.zeros(LANES, np.float32)
    half = MLA_ROPE // 2
    for i in range(half):
        src[i], rot[i], sign[i] = i, i, -1.0
        src[64 + i], rot[64 + i], sign[64 + i] = half + i, i, 1.0
    return src, rot, sign


def _take_cols(w, src):
    src = [int(v) for v in src]
    parts, start = [], 0
    for i in range(1, len(src) + 1):
        same_run = i < len(src) and ((src[i] < 0 and src[i - 1] < 0) or (src[i - 1] >= 0 and src[i] == src[i - 1] + 1))
        if not same_run:
            width = i - start
            parts.append(jnp.zeros((w.shape[0], width), w.dtype) if src[start] < 0
                         else w[:, src[start]:src[start] + width])
            start = i
    return jnp.concatenate(parts, axis=1)


def _rope_tables(seq, dim, rot, sign):
    inv = ROPE_THETA ** (-jnp.arange(0, dim, 2, dtype=F32) / dim)
    ang = jnp.arange(seq, dtype=F32)[:, None] * _take_cols(inv[None, :], rot)
    return jnp.cos(ang), jnp.sin(ang) * jnp.asarray(sign)


def _prep_layer(l, ffn1_w1, ffn1_w3, ffn1_w2, w_in, mla_w_uq, mla_w_ukv, w_o, ffn2_w1, ffn2_w3, ffn2_w2):
    da_src, _, _, _ = _da_lane_layout()
    ml_src, _, _ = _mla_lane_layout()
    head_src = np.concatenate([h * 2 * DA_QK_DIM + da_src for h in range(DA_HEADS)])
    o1, o2, o3 = DA_Q_W, 2 * DA_Q_W, 2 * DA_Q_W + DA_V_W
    o5 = o3 + MLA_Q_RANK + MLA_KV_RANK
    wi = w_in[l].astype(BF16)
    win = jnp.concatenate([_take_cols(wi[:, :o1], head_src),
                           _take_cols(wi[:, o1:o2], head_src),
                           wi[:, o2:o5],
                           _take_cols(wi[:, o5:], ml_src)], axis=1)
    per_q = MLA_NOPE + MLA_ROPE
    uq_src = np.concatenate([np.concatenate([h * per_q + np.arange(MLA_NOPE),
                                             np.where(ml_src >= 0, h * per_q + MLA_NOPE + ml_src, -1)])
                             for h in range(MLA_HEADS)])
    per_kv = MLA_NOPE + MLA_V
    ukv_src = np.concatenate([h * per_kv + np.arange(MLA_NOPE) for h in range(MLA_HEADS)]
                             + [h * per_kv + MLA_NOPE + np.arange(MLA_V) for h in range(MLA_HEADS)])
    return dict(
        f1=(ffn1_w1[l].astype(BF16), ffn1_w3[l].astype(BF16), ffn1_w2[l].astype(BF16)),
        f2=(ffn2_w1[l].astype(BF16), ffn2_w3[l].astype(BF16), ffn2_w2[l].astype(BF16)),
        win=win,
        wuq=_take_cols(mla_w_uq[l].astype(BF16), uq_src),
        wukv=_take_cols(mla_w_ukv[l].astype(BF16), ukv_src),
        wo=w_o[l].astype(BF16),
    )


def _encode_group(x, mods, prepped, small, final_norm_g, tabs):
    batch, seq, d = x.shape
    xt = x.reshape(batch * seq, d)
    da_mask = _da_lane_layout()[3]
    masks = (jnp.asarray(da_mask[0:1]), jnp.asarray(da_mask[1:2]))
    depth = len(prepped)
    row = lambda v: v.reshape(1, -1)
    for l in range(depth):
        p, s, mod = prepped[l], small[l], mods[l]
        lambda_init = 0.8 - 0.6 * math.exp(-0.3 * l)
        xt = _ffn(xt, mod, row(s["ffn1_norm"]), *p["f1"], row(s["ffn1_norm"]),
                  seq=seq, mod_base=0, final_norm=False)
        qda, k1, k2, vtda, qml, kml, vtml = _inproj(
            xt, mod, row(s["attn_norm"]), p["win"], tabs, masks, row(s["mla_q_norm"]), row(s["mla_kv_norm"]),
            p["wuq"], p["wukv"], batch=batch, seq=seq)
        o_da = _da_attn(s["lam_vecs"], row(s["da_subln"]), masks, qda, k1, k2, vtda, lambda_init=lambda_init)
        o_mla = _mla_attn(qml, kml, vtml)
        xt = _outproj(xt, mod, o_da.reshape(batch * seq, -1), o_mla.reshape(batch * seq, -1),
                      p["wo"], seq=seq)
        last = l == depth - 1
        xt = _ffn(xt, mod, row(s["ffn2_norm"]), *p["f2"], row(final_norm_g),
                  seq=seq, mod_base=6, final_norm=last)
    return xt.reshape(batch, seq, d)


def kernel(x_prompt, x_sample, c_prompt, c_sample, ffn1_norm, ffn1_w1, ffn1_w3, ffn1_w2, attn_norm, w_in,
           da_lambda_q1, da_lambda_k1, da_lambda_q2, da_lambda_k2, da_subln, mla_q_norm, mla_w_uq, mla_kv_norm,
           mla_w_ukv, w_o, ffn2_norm, ffn2_w1, ffn2_w3, ffn2_w2, w_ada, b_ada, final_norm):
    depth = w_in.shape[0]
    nb_p, nb_s = c_prompt.shape[0], c_sample.shape[0]
    c_all = jnp.concatenate([c_prompt, c_sample], axis=0)
    rows = -(-c_all.shape[0] // 8) * 8
    c_pad = jnp.pad(c_all, ((0, rows - c_all.shape[0]), (0, 0)))

    prepped, small, mods_p, mods_s = [], [], [], []
    for l in range(depth):
        prepped.append(_prep_layer(l, ffn1_w1, ffn1_w3, ffn1_w2, w_in, mla_w_uq, mla_w_ukv, w_o,
                                   ffn2_w1, ffn2_w3, ffn2_w2))
        small.append(dict(
            ffn1_norm=ffn1_norm[l], attn_norm=attn_norm[l], ffn2_norm=ffn2_norm[l], da_subln=da_subln[l],
            mla_q_norm=mla_q_norm[l], mla_kv_norm=mla_kv_norm[l],
            lam_vecs=jnp.stack([da_lambda_q1[l], da_lambda_k1[l], da_lambda_q2[l], da_lambda_k2[l]])))
        m = _ada(c_pad, w_ada[l], b_ada[l].reshape(1, -1))
        mods_p.append(m[:nb_p].reshape(nb_p, N_MOD, D_MODEL))
        mods_s.append(m[nb_p:nb_p + nb_s].reshape(nb_s, N_MOD, D_MODEL))

    _, da_rot, da_sign, _ = _da_lane_layout()
    _, ml_rot, ml_sign = _mla_lane_layout()
    max_seq = max(x_prompt.shape[1], x_sample.shape[1])
    tabs = _rope_tables(max_seq, DA_ROT, da_rot, da_sign) + _rope_tables(max_seq, MLA_ROPE, ml_rot, ml_sign)

    y_prompt = _encode_group(x_prompt, mods_p, prepped, small, final_norm, tabs)
    y_sample = _encode_group(x_sample, mods_s, prepped, small, final_norm, tabs)
    return (y_prompt, y_sample)
```

```python
import functools
import math

import jax
import jax.numpy as jnp
import numpy as np
from jax import lax
from jax.experimental import pallas as pl
from jax.experimental.pallas import tpu as pltpu

D_MODEL = 2048
D_FF = 5632
N_MOD = 9
DA_HEADS = 8
DA_QK_DIM = 64
DA_V_DIM = 128
DA_ROT = 16
DA_Q_W = DA_HEADS * 2 * DA_QK_DIM
DA_V_W = DA_HEADS * DA_V_DIM
MLA_HEADS = 8
MLA_NOPE = 128
MLA_ROPE = 64
MLA_V = 128
MLA_Q_RANK = 512
MLA_KV_RANK = 256
ROPE_THETA = 500000.0
EPS = 1e-6
LANES = 128
MLA_QK_W = 2 * LANES
VMEM_LIMIT = 56 * 1024 * 1024
FFN_VMEM_LIMIT = 60 * 1024 * 1024
NEG_BIG = -1e30
LOG2E = math.log2(math.e)
SHIFT_MAX_BOUND = 48.0
DA_BIAS_LANES = (8, 0)
MLA_BIAS_LANE = LANES + 32

FFN_TM, FFN_TF = 1024, 512
FFN_ROW_CHUNK = 256
PROJ_TM = 256
OUT_TM = 512
DA_TQ, DA_TK = 1024, 1024
MLA_TQ, MLA_TK = 1024, 2048
MLA_QSTREAMS = 1
ATT_QSUB = 2
SHIFT_UNROLL = 4
ADA_TN = 1024

BF16 = jnp.bfloat16
F32 = jnp.float32


def _params(*sem, vmem=VMEM_LIMIT):
    return pltpu.CompilerParams(dimension_semantics=sem, vmem_limit_bytes=vmem)


def _rms(x, g):
    return x * lax.rsqrt(jnp.mean(x * x, axis=-1, keepdims=True) + EPS) * g


def _silu(x):
    return x / (1.0 + jnp.exp(-x))


def _const_spec(shape):
    zeros = (0,) * len(shape)
    return pl.BlockSpec(shape, lambda *_: zeros, pipeline_mode=pl.Buffered(1))


def _ada_body(c_ref, w_ref, b_ref, o_ref):
    a = _silu(c_ref[...]).astype(BF16)
    o_ref[...] = jnp.dot(a, w_ref[...].astype(BF16), preferred_element_type=F32) + b_ref[...]


def _ada(c_pad, w, b):
    rows, d = c_pad.shape
    n = w.shape[1]
    return pl.pallas_call(
        _ada_body,
        grid=(n // ADA_TN,),
        in_specs=[pl.BlockSpec((rows, d), lambda j: (0, 0)),
                  pl.BlockSpec((d, ADA_TN), lambda j: (0, j)),
                  pl.BlockSpec((1, ADA_TN), lambda j: (0, j))],
        out_specs=pl.BlockSpec((rows, ADA_TN), lambda j: (0, j)),
        out_shape=jax.ShapeDtypeStruct((rows, n), F32),
        compiler_params=_params("arbitrary"),
        name="ada",
    )(c_pad, w, b)


def _ffn_body(x_ref, mod_ref, g_ref, w1_ref, w3_ref, w2_ref, fg_ref, o_ref, h_ref, *, mod_base, final_norm):
    j = pl.program_id(1)
    last = pl.num_programs(1) - 1
    chunks = [pl.ds(r, FFN_ROW_CHUNK) for r in range(0, x_ref.shape[0], FFN_ROW_CHUNK)]

    def swiglu(h):
        a = jnp.dot(h, w1_ref[...], preferred_element_type=F32)
        b = jnp.dot(h, w3_ref[...], preferred_element_type=F32)
        return jnp.dot((_silu(a) * b).astype(BF16), w2_ref[...], preferred_element_type=F32)

    @pl.when(j == 0)
    def _():
        shift = mod_ref[mod_base:mod_base + 1, :]
        scale = mod_ref[mod_base + 1:mod_base + 2, :]
        for rows in chunks:
            h = (_rms(x_ref[rows, :], g_ref[...]) * (1.0 + scale) + shift).astype(BF16)
            h_ref[rows, :] = h
            o_ref[rows, :] = swiglu(h)

    @pl.when(jnp.logical_and(j > 0, j < last))
    def _():
        o_ref[...] += swiglu(h_ref[...])

    @pl.when(j == last)
    def _():
        gate = mod_ref[mod_base + 2:mod_base + 3, :]
        for rows in chunks:
            y = x_ref[rows, :] + 0.5 * gate * (o_ref[rows, :] + swiglu(h_ref[rows, :]))
            if final_norm:
                y = _rms(y, fg_ref[...])
            o_ref[rows, :] = y


def _ffn(x, mod, g, w1, w3, w2, fg, *, seq, mod_base, final_norm):
    t, d = x.shape
    f = w1.shape[1]
    tm = min(FFN_TM, seq)
    tf = FFN_TF
    assert f // tf >= 2, "the first and the last hidden-dim step must be different steps"
    per_seq = seq // tm
    return pl.pallas_call(
        functools.partial(_ffn_body, mod_base=mod_base, final_norm=final_norm),
        grid=(t // tm, f // tf),
        in_specs=[pl.BlockSpec((tm, d), lambda i, j: (i, 0)),
                  pl.BlockSpec((None, N_MOD, d), lambda i, j: (i // per_seq, 0, 0)),
                  pl.BlockSpec((1, d), lambda i, j: (0, 0)),
                  pl.BlockSpec((d, tf), lambda i, j: (0, j)),
                  pl.BlockSpec((d, tf), lambda i, j: (0, j)),
                  pl.BlockSpec((tf, d), lambda i, j: (j, 0)),
                  pl.BlockSpec((1, d), lambda i, j: (0, 0))],
        out_specs=pl.BlockSpec((tm, d), lambda i, j: (i, 0)),
        out_shape=jax.ShapeDtypeStruct((t, d), F32),
        scratch_shapes=[pltpu.VMEM((tm, d), BF16)],
        compiler_params=_params("arbitrary", "arbitrary", vmem=FFN_VMEM_LIMIT),
        name="ffn_final" if final_norm else "ffn",
    )(x, mod, g, w1, w3, w2, fg)


def _rope(t, cos, sin_signed):
    return t * cos + pltpu.roll(t, LANES // 2, axis=1) * sin_signed


def _inproj_body(x_ref, mod_ref, g_ref, win_ref, cda_ref, sda_ref, cml_ref, sml_ref, m1_ref, m2_ref,
                 gq_ref, gkv_ref, wuq_ref, wukv_ref,
                 qda_ref, k1_ref, k2_ref, vtda_ref, qml_ref, kml_ref, vtml_ref):
    shift = mod_ref[3:4, :]
    scale = mod_ref[4:5, :]
    h = (_rms(x_ref[...], g_ref[...]) * (1.0 + scale) + shift).astype(BF16)
    cda, sda = cda_ref[...], sda_ref[...]
    cml, sml = cml_ref[...], sml_ref[...]
    mask1, mask2 = m1_ref[...], m2_ref[...]
    da_scale = DA_QK_DIM ** -0.5 * LOG2E
    mla_scale = (MLA_NOPE + MLA_ROPE) ** -0.5 * LOG2E

    def tile(a, k):
        return a[:, k * LANES:(k + 1) * LANES]

    def one_hot(lane):
        return (lax.broadcasted_iota(jnp.int32, (1, LANES), 1) == lane).astype(F32)

    pq = jnp.dot(h, win_ref[:, 0:DA_Q_W], preferred_element_type=F32)
    for hd in range(DA_HEADS):
        qda_ref[hd] = (_rope(tile(pq, hd), cda, sda) * da_scale).astype(BF16)
    pk = jnp.dot(h, win_ref[:, DA_Q_W:2 * DA_Q_W], preferred_element_type=F32)
    for hd in range(DA_HEADS):
        r = _rope(tile(pk, hd), cda, sda)
        k1_ref[hd] = (r * mask1 + one_hot(DA_BIAS_LANES[0])).astype(BF16)
        k2_ref[hd] = (r * mask2 + one_hot(DA_BIAS_LANES[1])).astype(BF16)
    pv = jnp.dot(h, win_ref[:, 2 * DA_Q_W:2 * DA_Q_W + DA_V_W], preferred_element_type=F32)
    for hd in range(DA_HEADS):
        vtda_ref[hd] = tile(pv, hd).T.astype(BF16)

    o3 = 2 * DA_Q_W + DA_V_W
    pr = jnp.dot(h, win_ref[:, o3:], preferred_element_type=F32)
    cq = _rms(pr[:, 0:MLA_Q_RANK], gq_ref[...]).astype(BF16)
    ckv = _rms(pr[:, MLA_Q_RANK:MLA_Q_RANK + MLA_KV_RANK], gkv_ref[...]).astype(BF16)
    kpe = (_rope(pr[:, MLA_Q_RANK + MLA_KV_RANK:], cml, sml) + one_hot(MLA_BIAS_LANE - LANES)).astype(BF16)

    qm = jnp.dot(cq, wuq_ref[...], preferred_element_type=F32)
    for hd in range(MLA_HEADS):
        qml_ref[hd, :, 0:LANES] = (tile(qm, 2 * hd) * mla_scale).astype(BF16)
        qml_ref[hd, :, LANES:] = (_rope(tile(qm, 2 * hd + 1), cml, sml) * mla_scale).astype(BF16)
    kv = jnp.dot(ckv, wukv_ref[...], preferred_element_type=F32)
    for hd in range(MLA_HEADS):
        kml_ref[hd, :, 0:LANES] = tile(kv, hd).astype(BF16)
        kml_ref[hd, :, LANES:] = kpe
        vtml_ref[hd] = tile(kv, MLA_HEADS + hd).T.astype(BF16)


def _inproj(x, mod, g, win, tabs, masks, gq, gkv, wuq, wukv, *, batch, seq):
    t, d = x.shape
    tm = min(PROJ_TM, seq)
    per_seq = seq // tm
    row = lambda i: (i, 0)
    tab = lambda i: (i % per_seq, 0)
    head_rows = lambda i: (i // per_seq, 0, i % per_seq, 0)
    head_cols = lambda i: (i // per_seq, 0, 0, i % per_seq)
    hq = lambda w: pl.BlockSpec((None, DA_HEADS, tm, w), head_rows)
    vt = pl.BlockSpec((None, DA_HEADS, LANES, tm), head_cols)
    sd = jax.ShapeDtypeStruct
    return pl.pallas_call(
        _inproj_body,
        grid=(t // tm,),
        in_specs=[pl.BlockSpec((tm, d), row),
                  pl.BlockSpec((None, N_MOD, d), lambda i: (i // per_seq, 0, 0)),
                  _const_spec((1, d)),
                  _const_spec(win.shape),
                  pl.BlockSpec((tm, LANES), tab), pl.BlockSpec((tm, LANES), tab),
                  pl.BlockSpec((tm, LANES), tab), pl.BlockSpec((tm, LANES), tab),
                  _const_spec((1, LANES)), _const_spec((1, LANES)),
                  _const_spec((1, MLA_Q_RANK)), _const_spec((1, MLA_KV_RANK)),
                  _const_spec(wuq.shape), _const_spec(wukv.shape)],
        out_specs=[hq(LANES), hq(LANES), hq(LANES), vt, hq(MLA_QK_W), hq(MLA_QK_W), vt],
        out_shape=[sd((batch, DA_HEADS, seq, LANES), BF16), sd((batch, DA_HEADS, seq, LANES), BF16),
                   sd((batch, DA_HEADS, seq, LANES), BF16), sd((batch, DA_HEADS, LANES, seq), BF16),
                   sd((batch, MLA_HEADS, seq, MLA_QK_W), BF16), sd((batch, MLA_HEADS, seq, MLA_QK_W), BF16),
                   sd((batch, MLA_HEADS, LANES, seq), BF16)],
        compiler_params=_params("arbitrary"),
        name="inproj",
    )(x, mod, g, win, *tabs, *masks, gq, gkv, wuq, wukv)


class _Stream:
    def __init__(self, q_ref, q_mask, bias_lane, k_ref, vt_ref, acc_ref, sum_ref, qz_ref, k2max_ref, bufs, tk):
        self.q_ref, self.q_mask, self.bias_lane, self.k_ref, self.vt_ref = q_ref, q_mask, bias_lane, k_ref, vt_ref
        self.acc_ref, self.sum_ref, self.qz_ref, self.k2max_ref, self.bufs, self.tk = (
            acc_ref, sum_ref, qz_ref, k2max_ref, bufs, tk)

    def _keys(self, blk):
        return pl.ds(pl.multiple_of(blk * self.tk, self.tk), self.tk)

    def masked_q(self):
        q = self.q_ref[...].astype(F32)
        return q if self.q_mask is None else q * self.q_mask

    def store_key_norm(self, nblk):
        def blk(b, best):
            k = self.k_ref[self._keys(b), :].astype(F32)
            return jnp.maximum(best, jnp.max(jnp.sum(k * k, axis=1, keepdims=True), axis=0, keepdims=True))
        best = lax.fori_loop(0, nblk, blk, jnp.zeros((1, 1), F32))
        self.k2max_ref[...] = jnp.broadcast_to(best, self.k2max_ref.shape)

    def score_bound(self):
        q = self.masked_q()
        q2 = jnp.max(jnp.sum(q * q, axis=1, keepdims=True), axis=0, keepdims=True)
        return jnp.sqrt(q2 * self.k2max_ref[0:1, 0:1]) * 1.01 + 0.01

    def set_query(self, shift):
        q = self.masked_q()
        if shift is not None:
            lane = lax.broadcasted_iota(jnp.int32, q.shape, 1)
            q = jnp.where(lane == self.bias_lane, -shift, q)
        self.qz_ref[...] = q.astype(BF16)

    def _qk(self, blk):
        return lax.dot_general(self.k_ref[self._keys(blk), :], self.qz_ref[...], (((1,), (1,)), ((), ())),
                               preferred_element_type=F32)

    def _pv(self, blk, p):
        return jnp.dot(self.vt_ref[:, self._keys(blk)], p.astype(BF16), preferred_element_type=F32)

    def scores(self, blk, slot):
        s = self._qk(blk)
        self.bufs[slot][...] = s
        return jnp.max(s, axis=0, keepdims=True)

    def consume(self, blk, slot, blk_max, m, l):
        m_new = jnp.maximum(m, blk_max)
        p = jnp.exp2(self.bufs[slot][...] - m_new)
        alpha = jnp.exp2(m - m_new)
        l_new = alpha * l + jnp.sum(p, axis=0, keepdims=True)
        self.acc_ref[...] = alpha * self.acc_ref[...] + self._pv(blk, p)
        return m_new, l_new

    def accumulate(self, blk, part):
        p = jnp.exp2(self._qk(blk))
        self.acc_ref[...] += self._pv(blk, p)
        return part + jnp.sum(p.reshape(self.tk // 8, 8, p.shape[1]), axis=0)


def _flash_exact(streams, nblk, tq):
    m0 = jnp.full((1, tq), NEG_BIG, F32)
    l0 = jnp.zeros((1, tq), F32)
    for st in streams:
        st.set_query(None)
    first = tuple((m0, l0, st.scores(0, 0)) for st in streams)

    def pair(j, carry):
        b0 = 2 * j
        b2 = jnp.minimum(b0 + 2, nblk - 1)
        out = []
        for st, (m, l, max0) in zip(streams, carry):
            max1 = st.scores(b0 + 1, 1)
            m, l = st.consume(b0, 0, max0, m, l)
            max2 = st.scores(b2, 0)
            m, l = st.consume(b0 + 1, 1, max1, m, l)
            out.append((m, l, max2))
        return tuple(out)

    final = lax.fori_loop(0, nblk // 2, pair, first)
    for st, (_, l, _) in zip(streams, final):
        st.sum_ref[...] = l


def _flash_shifted(streams, bounds, nblk, tq):
    for st, bound in zip(streams, bounds):
        st.set_query(bound)

    unroll = SHIFT_UNROLL if nblk % SHIFT_UNROLL == 0 else 2

    def group(j, parts):
        for u in range(unroll):
            parts = tuple(st.accumulate(unroll * j + u, part) for st, part in zip(streams, parts))
        return parts

    parts = lax.fori_loop(0, nblk // unroll, group, tuple(jnp.zeros((8, tq), F32) for _ in streams))
    for st, part in zip(streams, parts):
        st.sum_ref[...] = jnp.sum(part, axis=0, keepdims=True)


def _flash(streams, nblk, tq, new_keys):
    assert nblk % 2 == 0

    @pl.when(new_keys)
    def _():
        for st in {id(st.k2max_ref): st for st in streams}.values():
            st.store_key_norm(nblk)

    for st in streams:
        st.acc_ref[...] = jnp.zeros(st.acc_ref.shape, F32)
    bounds = [st.score_bound() for st in streams]
    shift_ok = functools.reduce(jnp.maximum, bounds)[0, 0] <= SHIFT_MAX_BOUND

    @pl.when(shift_ok)
    def _():
        _flash_shifted(streams, bounds, nblk, tq)

    @pl.when(jnp.logical_not(shift_ok))
    def _():
        _flash_exact(streams, nblk, tq)


def _da_body(lam_ref, g_ref, m1_ref, m2_ref, q_ref, k1_ref, k2_ref, vt_ref, o_ref,
             acc_ref, sum_ref, qz_ref, k2max_ref, *bufs, tk, lambda_init):
    tq = q_ref.shape[0] // ATT_QSUB
    masks = (m1_ref[...], m2_ref[...])
    lam_scr = bufs[-1]
    new_head = pl.program_id(2) == 0

    @pl.when(new_head)
    def _():
        lq1, lk1, lq2, lk2 = (lam_ref[r:r + 1, :] for r in range(4))
        lam = (jnp.exp(jnp.sum(lq1 * lk1, axis=-1, keepdims=True))
               - jnp.exp(jnp.sum(lq2 * lk2, axis=-1, keepdims=True)) + lambda_init)
        lam_scr[...] = jnp.broadcast_to(lam, lam_scr.shape)

    def query_block(sub, carry):
        rows = pl.ds(pl.multiple_of(sub * tq, tq), tq)
        streams = [_Stream(q_ref.at[rows, :], masks[j], DA_BIAS_LANES[j], (k1_ref, k2_ref)[j], vt_ref, acc_ref.at[j],
                           sum_ref.at[j], qz_ref.at[j], k2max_ref.at[j], bufs[2 * j:2 * j + 2], tk) for j in range(2)]
        _flash(streams, k1_ref.shape[0] // tk, tq, jnp.logical_and(new_head, sub == 0))
        lam = lam_scr[0:1, 0:1]
        ot = acc_ref[0] / sum_ref[0] - lam * (acc_ref[1] / sum_ref[1])
        ot = ot * lax.rsqrt(jnp.mean(ot * ot, axis=0, keepdims=True) + EPS)
        o_ref[rows, :] = (ot.T * g_ref[...] * (1.0 - lambda_init)).astype(o_ref.dtype)
        return carry

    lax.fori_loop(0, ATT_QSUB, query_block, 0)


def _mla_body(q_ref, k_ref, vt_ref, o_ref, acc_ref, sum_ref, qz_ref, k2max_ref, *bufs, tk):
    nq = acc_ref.shape[0]
    tq = q_ref.shape[0] // (nq * ATT_QSUB)
    key_norm = k2max_ref.at[0]
    new_head = pl.program_id(2) == 0

    def query_blocks(sub, carry):
        rows = [pl.ds(pl.multiple_of((sub * nq + i) * tq, tq), tq) for i in range(nq)]
        streams = [_Stream(q_ref.at[rows[i], :], None, MLA_BIAS_LANE, k_ref, vt_ref, acc_ref.at[i], sum_ref.at[i],
                           qz_ref.at[i], key_norm, bufs[2 * i:2 * i + 2], tk) for i in range(nq)]
        _flash(streams, k_ref.shape[0] // tk, tq, jnp.logical_and(new_head, sub == 0))
        for i in range(nq):
            o_ref[rows[i], :] = (acc_ref[i] / sum_ref[i]).T.astype(o_ref.dtype)
        return carry

    lax.fori_loop(0, ATT_QSUB, query_blocks, 0)


def _attn_specs(batch, heads, seq, qk_w, nq, tq_max, tk_max):
    tq = min(tq_max, seq // nq)
    tk = min(tk_max, seq)
    q_spec = pl.BlockSpec((None, None, nq * tq, qk_w), lambda b, h, i: (b, h, i, 0))
    k_spec = pl.BlockSpec((None, None, seq, qk_w), lambda b, h, i: (b, h, 0, 0))
    vt_spec = pl.BlockSpec((None, None, LANES, seq), lambda b, h, i: (b, h, 0, 0))
    o_spec = pl.BlockSpec((None, nq * tq, LANES), lambda b, h, i: (b, i, h))
    o_shape = jax.ShapeDtypeStruct((batch, seq, heads * LANES), BF16)
    return tq, tk, (batch, heads, seq // (nq * tq)), q_spec, k_spec, vt_spec, o_spec, o_shape


def _attn_scratch(streams, key_sets, tq, tk, qk_w):
    return ([pltpu.VMEM((streams, LANES, tq), F32),
             pltpu.VMEM((streams, 1, tq), F32),
             pltpu.VMEM((streams, tq, qk_w), BF16),
             pltpu.VMEM((key_sets, 8, LANES), F32)]
            + [pltpu.VMEM((tk, tq), F32)] * (2 * streams))


def _da_attn(lam_vecs, subln_g, masks, q, k1, k2, vt, *, lambda_init):
    batch, heads, seq, _ = q.shape
    tq, tk, grid, q_spec, k_spec, vt_spec, o_spec, o_shape = _attn_specs(
        batch, heads, seq, LANES, ATT_QSUB, DA_TQ, DA_TK)
    lane_row = pl.BlockSpec((1, LANES), lambda b, h, i: (0, 0))
    return pl.pallas_call(
        functools.partial(_da_body, tk=tk, lambda_init=lambda_init),
        grid=grid,
        in_specs=[pl.BlockSpec(lam_vecs.shape, lambda b, h, i: (0, 0)), lane_row, lane_row, lane_row,
                  q_spec, k_spec, k_spec, vt_spec],
        out_specs=o_spec,
        out_shape=o_shape,
        scratch_shapes=_attn_scratch(2, 2, tq, tk, LANES) + [pltpu.VMEM((8, LANES), F32)],
        compiler_params=_params("arbitrary", "arbitrary", "arbitrary"),
        name="da_attn",
    )(lam_vecs, subln_g, *masks, q, k1, k2, vt)


def _mla_attn(q, k, vt):
    batch, heads, seq, qk_w = q.shape
    nq = MLA_QSTREAMS
    tq, tk, grid, q_spec, k_spec, vt_spec, o_spec, o_shape = _attn_specs(
        batch, heads, seq, qk_w, nq * ATT_QSUB, MLA_TQ, MLA_TK)
    return pl.pallas_call(
        functools.partial(_mla_body, tk=tk),
        grid=grid,
        in_specs=[q_spec, k_spec, vt_spec],
        out_specs=o_spec,
        out_shape=o_shape,
        scratch_shapes=_attn_scratch(nq, 1, tq, tk, qk_w),
        compiler_params=_params("arbitrary", "arbitrary", "arbitrary"),
        name="mla_attn",
    )(q, k, vt)


def _outproj_body(x_ref, mod_ref, oda_ref, oml_ref, wo_ref, o_ref):
    split = oda_ref.shape[1]
    mix = (jnp.dot(oda_ref[...], wo_ref[0:split, :], preferred_element_type=F32)
           + jnp.dot(oml_ref[...], wo_ref[split:, :], preferred_element_type=F32))
    o_ref[...] = x_ref[...] + mod_ref[5:6, :] * mix


def _outproj(x, mod, o_da, o_mla, wo, *, seq):
    t, d = x.shape
    tm = min(OUT_TM, seq)
    per_seq = seq // tm
    row = lambda i: (i, 0)
    return pl.pallas_call(
        _outproj_body,
        grid=(t // tm,),
        in_specs=[pl.BlockSpec((tm, d), row),
                  pl.BlockSpec((None, N_MOD, d), lambda i: (i // per_seq, 0, 0)),
                  pl.BlockSpec((tm, o_da.shape[1]), row),
                  pl.BlockSpec((tm, o_mla.shape[1]), row),
                  _const_spec(wo.shape)],
        out_specs=pl.BlockSpec((tm, d), row),
        out_shape=jax.ShapeDtypeStruct((t, d), F32),
        compiler_params=_params("arbitrary"),
        name="outproj",
    )(x, mod, o_da, o_mla, wo)


def _da_lane_layout():
    src = np.zeros(LANES, np.int32)
    rot = np.full(LANES, -1, np.int32)
    sign = np.zeros(LANES, np.float32)
    mask = np.zeros((2, LANES), np.float32)
    half = DA_ROT // 2
    for j in range(2):
        base = j * DA_QK_DIM
        for i in range(half):
            src[j * half + i] = base + i
            rot[j * half + i] = i
            sign[j * half + i] = -1.0
            src[64 + j * half + i] = base + half + i
            rot[64 + j * half + i] = i
            sign[64 + j * half + i] = 1.0
            mask[j, j * half + i] = mask[j, 64 + j * half + i] = 1.0
        plain = DA_QK_DIM - DA_ROT
        start = DA_ROT + j * 64
        for i in range(plain):
            src[start + i] = base + DA_ROT + i
            mask[j, start + i] = 1.0
    return src, rot, sign, mask


def _mla_lane_layout():
    src = np.full(LANES, -1, np.int32)
    rot = np.full(LANES, -1, np.int32)
    sign = np.zeros(LANES, np.float32)
    half = MLA_ROPE // 2
    for i in range(half):
        src[i], rot[i], sign[i] = i, i, -1.0
        src[64 + i], rot[64 + i], sign[64 + i] = half + i, i, 1.0
    return src, rot, sign


def _take_cols(w, src, axis=1):
    src = [int(v) for v in src]
    parts, start = [], 0
    for i in range(1, len(src) + 1):
        same_run = i < len(src) and ((src[i] < 0 and src[i - 1] < 0) or (src[i - 1] >= 0 and src[i] == src[i - 1] + 1))
        if not same_run:
            width = i - start
            if src[start] < 0:
                shape = list(w.shape)
                shape[axis] = width
                parts.append(jnp.zeros(shape, w.dtype))
            else:
                parts.append(lax.slice_in_dim(w, src[start], src[start] + width, axis=axis))
            start = i
    return jnp.concatenate(parts, axis=axis)


def _rope_tables(seq, dim, rot, sign):
    inv = ROPE_THETA ** (-jnp.arange(0, dim, 2, dtype=F32) / dim)
    ang = jnp.arange(seq, dtype=F32)[:, None] * inv[None, :]
    off = jnp.asarray((rot < 0).astype(np.float32))
    return _take_cols(jnp.cos(ang), rot) + off, _take_cols(jnp.sin(ang), rot) * jnp.asarray(sign)


def _prep_layer(l, ffn1_w1, ffn1_w3, ffn1_w2, w_in, mla_w_uq, mla_w_ukv, w_o, ffn2_w1, ffn2_w3, ffn2_w2):
    da_src, _, _, _ = _da_lane_layout()
    ml_src, _, _ = _mla_lane_layout()
    head_src = np.concatenate([h * 2 * DA_QK_DIM + da_src for h in range(DA_HEADS)])
    o1, o2, o3 = DA_Q_W, 2 * DA_Q_W, 2 * DA_Q_W + DA_V_W
    o5 = o3 + MLA_Q_RANK + MLA_KV_RANK
    wt = jnp.transpose(w_in[l]).astype(BF16)
    win = jnp.transpose(jnp.concatenate([_take_cols(wt[:o1], head_src, axis=0),
                                         _take_cols(wt[o1:o2], head_src, axis=0),
                                         wt[o2:o5],
                                         _take_cols(wt[o5:], ml_src, axis=0)], axis=0))
    per_q = MLA_NOPE + MLA_ROPE
    uq_src = np.concatenate([np.concatenate([h * per_q + np.arange(MLA_NOPE),
                                             np.where(ml_src >= 0, h * per_q + MLA_NOPE + ml_src, -1)])
                             for h in range(MLA_HEADS)])
    per_kv = MLA_NOPE + MLA_V
    ukv_src = np.concatenate([h * per_kv + np.arange(MLA_NOPE) for h in range(MLA_HEADS)]
                             + [h * per_kv + MLA_NOPE + np.arange(MLA_V) for h in range(MLA_HEADS)])
    return dict(
        f1=(ffn1_w1[l].astype(BF16), ffn1_w3[l].astype(BF16), ffn1_w2[l].astype(BF16)),
        f2=(ffn2_w1[l].astype(BF16), ffn2_w3[l].astype(BF16), ffn2_w2[l].astype(BF16)),
        win=win,
        wuq=_take_cols(mla_w_uq[l].astype(BF16), uq_src),
        wukv=_take_cols(mla_w_ukv[l].astype(BF16), ukv_src),
        wo=w_o[l].astype(BF16),
    )


def _encode_group(x, mods, prepped, small, final_norm_g, tabs):
    batch, seq, d = x.shape
    xt = x.reshape(batch * seq, d)
    da_mask = _da_lane_layout()[3]
    masks = (jnp.asarray(da_mask[0:1]), jnp.asarray(da_mask[1:2]))
    depth = len(prepped)
    row = lambda v: v.reshape(1, -1)
    for l in range(depth):
        p, s, mod = prepped[l], small[l], mods[l]
        lambda_init = 0.8 - 0.6 * math.exp(-0.3 * l)
        xt = _ffn(xt, mod, row(s["ffn1_norm"]), *p["f1"], row(s["ffn1_norm"]),
                  seq=seq, mod_base=0, final_norm=False)
        qda, k1, k2, vtda, qml, kml, vtml = _inproj(
            xt, mod, row(s["attn_norm"]), p["win"], tabs, masks, row(s["mla_q_norm"]), row(s["mla_kv_norm"]),
            p["wuq"], p["wukv"], batch=batch, seq=seq)
        o_da = _da_attn(s["lam_vecs"], row(s["da_subln"]), masks, qda, k1, k2, vtda, lambda_init=lambda_init)
        o_mla = _mla_attn(qml, kml, vtml)
        xt = _outproj(xt, mod, o_da.reshape(batch * seq, -1), o_mla.reshape(batch * seq, -1),
                      p["wo"], seq=seq)
        last = l == depth - 1
        xt = _ffn(xt, mod, row(s["ffn2_norm"]), *p["f2"], row(final_norm_g),
                  seq=seq, mod_base=6, final_norm=last)
    return xt.reshape(batch, seq, d)


def kernel(x_prompt, x_sample, c_prompt, c_sample, ffn1_norm, ffn1_w1, ffn1_w3, ffn1_w2, attn_norm, w_in,
           da_lambda_q1, da_lambda_k1, da_lambda_q2, da_lambda_k2, da_subln, mla_q_norm, mla_w_uq, mla_kv_norm,
           mla_w_ukv, w_o, ffn2_norm, ffn2_w1, ffn2_w3, ffn2_w2, w_ada, b_ada, final_norm):
    depth = w_in.shape[0]
    nb_p, nb_s = c_prompt.shape[0], c_sample.shape[0]
    c_all = jnp.concatenate([c_prompt, c_sample], axis=0)
    rows = -(-c_all.shape[0] // 8) * 8
    c_pad = jnp.pad(c_all, ((0, rows - c_all.shape[0]), (0, 0)))

    prepped, small, mods_p, mods_s = [], [], [], []
    for l in range(depth):
        prepped.append(_prep_layer(l, ffn1_w1, ffn1_w3, ffn1_w2, w_in, mla_w_uq, mla_w_ukv, w_o,
                                   ffn2_w1, ffn2_w3, ffn2_w2))
        small.append(dict(
            ffn1_norm=ffn1_norm[l], attn_norm=attn_norm[l], ffn2_norm=ffn2_norm[l], da_subln=da_subln[l],
            mla_q_norm=mla_q_norm[l], mla_kv_norm=mla_kv_norm[l],
            lam_vecs=jnp.stack([da_lambda_q1[l], da_lambda_k1[l], da_lambda_q2[l], da_lambda_k2[l]])))
        m = _ada(c_pad, w_ada[l], b_ada[l].reshape(1, -1))
        mods_p.append(m[:nb_p].reshape(nb_p, N_MOD, D_MODEL))
        mods_s.append(m[nb_p:nb_p + nb_s].reshape(nb_s, N_MOD, D_MODEL))

    _, da_rot, da_sign, _ = _da_lane_layout()
    _, ml_rot, ml_sign = _mla_lane_layout()
    max_seq = max(x_prompt.shape[1], x_sample.shape[1])
    tabs = _rope_tables(max_seq, DA_ROT, da_rot, da_sign) + _rope_tables(max_seq, MLA_ROPE, ml_rot, ml_sign)

    y_prompt = _encode_group(x_prompt, mods_p, prepped, small, final_norm, tabs)
    y_sample = _encode_group(x_sample, mods_s, prepped, small, final_norm, tabs)
    return (y_prompt, y_sample)
```

```python
import functools
import math

import jax
import jax.numpy as jnp
import numpy as np
from jax import lax
from jax.experimental import pallas as pl
from jax.experimental.pallas import tpu as pltpu

D_MODEL = 2048
D_FF = 5632
N_MOD = 9
DA_HEADS = 8
DA_QK_DIM = 64
DA_V_DIM = 128
DA_ROT = 16
DA_Q_W = DA_HEADS * 2 * DA_QK_DIM
DA_V_W = DA_HEADS * DA_V_DIM
MLA_HEADS = 8
MLA_NOPE = 128
MLA_ROPE = 64
MLA_V = 128
MLA_Q_RANK = 512
MLA_KV_RANK = 256
ROPE_THETA = 500000.0
EPS = 1e-6
LANES = 128
MLA_QK_W = 2 * LANES
VMEM_LIMIT = 56 * 1024 * 1024
FFN_VMEM_LIMIT = 60 * 1024 * 1024
NEG_BIG = -1e30
LOG2E = math.log2(math.e)
SHIFT_MAX_BOUND = 40.0
DA_BIAS_LANES = (8, 0)
MLA_BIAS_LANE = LANES + 32

FFN_TM, FFN_TF = 1024, 512
FFN_ROW_CHUNK = 256
PROJ_TM = 256
OUT_TM = 512
DA_TQ, DA_TK = 1024, 1024
MLA_TQ, MLA_TK = 1024, 2048
MLA_QSTREAMS = 1
ATT_QSUB = 2
SHIFT_UNROLL = 4
KEY_NORM_UNROLL = 4
ADA_TN = 1024

BF16 = jnp.bfloat16
F32 = jnp.float32


def _params(*sem, vmem=VMEM_LIMIT):
    return pltpu.CompilerParams(dimension_semantics=sem, vmem_limit_bytes=vmem)


def _rms(x, g):
    return x * lax.rsqrt(jnp.mean(x * x, axis=-1, keepdims=True) + EPS) * g


def _silu(x):
    return x / (1.0 + jnp.exp(-x))


def _const_spec(shape):
    zeros = (0,) * len(shape)
    return pl.BlockSpec(shape, lambda *_: zeros, pipeline_mode=pl.Buffered(1))


def _ada_body(c_ref, w_ref, b_ref, o_ref):
    a = _silu(c_ref[...]).astype(BF16)
    o_ref[...] = jnp.dot(a, w_ref[...].astype(BF16), preferred_element_type=F32) + b_ref[...]


def _ada(c_pad, w, b):
    rows, d = c_pad.shape
    n = w.shape[1]
    return pl.pallas_call(
        _ada_body,
        grid=(n // ADA_TN,),
        in_specs=[pl.BlockSpec((rows, d), lambda j: (0, 0)),
                  pl.BlockSpec((d, ADA_TN), lambda j: (0, j)),
                  pl.BlockSpec((1, ADA_TN), lambda j: (0, j))],
        out_specs=pl.BlockSpec((rows, ADA_TN), lambda j: (0, j)),
        out_shape=jax.ShapeDtypeStruct((rows, n), F32),
        compiler_params=_params("arbitrary"),
        name="ada",
    )(c_pad, w, b)


def _ffn_body(x_ref, mod_ref, g_ref, w1_ref, w3_ref, w2_ref, fg_ref, o_ref, h_ref, *, mod_base, final_norm):
    j = pl.program_id(1)
    last = pl.num_programs(1) - 1
    chunks = [pl.ds(r, FFN_ROW_CHUNK) for r in range(0, x_ref.shape[0], FFN_ROW_CHUNK)]

    def swiglu(h):
        a = jnp.dot(h, w1_ref[...], preferred_element_type=F32)
        b = jnp.dot(h, w3_ref[...], preferred_element_type=F32)
        return jnp.dot((_silu(a) * b).astype(BF16), w2_ref[...], preferred_element_type=F32)

    @pl.when(j == 0)
    def _():
        shift = mod_ref[mod_base:mod_base + 1, :]
        scale = mod_ref[mod_base + 1:mod_base + 2, :]
        for rows in chunks:
            h = (_rms(x_ref[rows, :], g_ref[...]) * (1.0 + scale) + shift).astype(BF16)
            h_ref[rows, :] = h
            o_ref[rows, :] = swiglu(h)

    @pl.when(jnp.logical_and(j > 0, j < last))
    def _():
        o_ref[...] += swiglu(h_ref[...])

    @pl.when(j == last)
    def _():
        gate = mod_ref[mod_base + 2:mod_base + 3, :]
        for rows in chunks:
            y = x_ref[rows, :] + 0.5 * gate * (o_ref[rows, :] + swiglu(h_ref[rows, :]))
            if final_norm:
                y = _rms(y, fg_ref[...])
            o_ref[rows, :] = y


def _ffn(x, mod, g, w1, w3, w2, fg, *, seq, mod_base, final_norm):
    t, d = x.shape
    f = w1.shape[1]
    tm = min(FFN_TM, seq)
    tf = FFN_TF
    assert f // tf >= 2, "the first and the last hidden-dim step must be different steps"
    per_seq = seq // tm
    return pl.pallas_call(
        functools.partial(_ffn_body, mod_base=mod_base, final_norm=final_norm),
        grid=(t // tm, f // tf),
        in_specs=[pl.BlockSpec((tm, d), lambda i, j: (i, 0)),
                  pl.BlockSpec((None, N_MOD, d), lambda i, j: (i // per_seq, 0, 0)),
                  pl.BlockSpec((1, d), lambda i, j: (0, 0)),
                  pl.BlockSpec((d, tf), lambda i, j: (0, j)),
                  pl.BlockSpec((d, tf), lambda i, j: (0, j)),
                  pl.BlockSpec((tf, d), lambda i, j: (j, 0)),
                  pl.BlockSpec((1, d), lambda i, j: (0, 0))],
        out_specs=pl.BlockSpec((tm, d), lambda i, j: (i, 0)),
        out_shape=jax.ShapeDtypeStruct((t, d), F32),
        scratch_shapes=[pltpu.VMEM((tm, d), BF16)],
        compiler_params=_params("arbitrary", "arbitrary", vmem=FFN_VMEM_LIMIT),
        name="ffn_final" if final_norm else "ffn",
    )(x, mod, g, w1, w3, w2, fg)


def _rope(t, cos, sin_signed):
    return t * cos + pltpu.roll(t, LANES // 2, axis=1) * sin_signed


def _inproj_body(x_ref, mod_ref, g_ref, win_ref, cda_ref, sda_ref, cml_ref, sml_ref, m1_ref, m2_ref,
                 gq_ref, gkv_ref, wuq_ref, wukv_ref,
                 qda_ref, k1_ref, k2_ref, vtda_ref, qml_ref, kml_ref, vtml_ref):
    shift = mod_ref[3:4, :]
    scale = mod_ref[4:5, :]
    h = (_rms(x_ref[...], g_ref[...]) * (1.0 + scale) + shift).astype(BF16)
    cda, sda = cda_ref[...], sda_ref[...]
    cml, sml = cml_ref[...], sml_ref[...]
    mask1, mask2 = m1_ref[...], m2_ref[...]
    da_scale = DA_QK_DIM ** -0.5 * LOG2E
    mla_scale = (MLA_NOPE + MLA_ROPE) ** -0.5 * LOG2E

    def tile(a, k):
        return a[:, k * LANES:(k + 1) * LANES]

    def one_hot(lane):
        return (lax.broadcasted_iota(jnp.int32, (1, LANES), 1) == lane).astype(F32)

    pq = jnp.dot(h, win_ref[:, 0:DA_Q_W], preferred_element_type=F32)
    for hd in range(DA_HEADS):
        qda_ref[hd] = (_rope(tile(pq, hd), cda, sda) * da_scale).astype(BF16)
    pk = jnp.dot(h, win_ref[:, DA_Q_W:2 * DA_Q_W], preferred_element_type=F32)
    for hd in range(DA_HEADS):
        r = _rope(tile(pk, hd), cda, sda)
        k1_ref[hd] = (r * mask1 + one_hot(DA_BIAS_LANES[0])).astype(BF16)
        k2_ref[hd] = (r * mask2 + one_hot(DA_BIAS_LANES[1])).astype(BF16)
    pv = jnp.dot(h, win_ref[:, 2 * DA_Q_W:2 * DA_Q_W + DA_V_W], preferred_element_type=F32)
    for hd in range(DA_HEADS):
        vtda_ref[hd] = tile(pv, hd).T.astype(BF16)

    o3 = 2 * DA_Q_W + DA_V_W
    pr = jnp.dot(h, win_ref[:, o3:], preferred_element_type=F32)
    cq = _rms(pr[:, 0:MLA_Q_RANK], gq_ref[...]).astype(BF16)
    ckv = _rms(pr[:, MLA_Q_RANK:MLA_Q_RANK + MLA_KV_RANK], gkv_ref[...]).astype(BF16)
    kpe = (_rope(pr[:, MLA_Q_RANK + MLA_KV_RANK:], cml, sml) + one_hot(MLA_BIAS_LANE - LANES)).astype(BF16)

    qm = jnp.dot(cq, wuq_ref[...], preferred_element_type=F32)
    for hd in range(MLA_HEADS):
        qml_ref[hd, :, 0:LANES] = (tile(qm, 2 * hd) * mla_scale).astype(BF16)
        qml_ref[hd, :, LANES:] = (_rope(tile(qm, 2 * hd + 1), cml, sml) * mla_scale).astype(BF16)
    kv = jnp.dot(ckv, wukv_ref[...], preferred_element_type=F32)
    for hd in range(MLA_HEADS):
        kml_ref[hd, :, 0:LANES] = tile(kv, hd).astype(BF16)
        kml_ref[hd, :, LANES:] = kpe
        vtml_ref[hd] = tile(kv, MLA_HEADS + hd).T.astype(BF16)


def _inproj(x, mod, g, win, tabs, masks, gq, gkv, wuq, wukv, *, batch, seq):
    t, d = x.shape
    tm = min(PROJ_TM, seq)
    per_seq = seq // tm
    row = lambda i: (i, 0)
    tab = lambda i: (i % per_seq, 0)
    head_rows = lambda i: (i // per_seq, 0, i % per_seq, 0)
    head_cols = lambda i: (i // per_seq, 0, 0, i % per_seq)
    hq = lambda w: pl.BlockSpec((None, DA_HEADS, tm, w), head_rows)
    vt = pl.BlockSpec((None, DA_HEADS, LANES, tm), head_cols)
    sd = jax.ShapeDtypeStruct
    return pl.pallas_call(
        _inproj_body,
        grid=(t // tm,),
        in_specs=[pl.BlockSpec((tm, d), row),
                  pl.BlockSpec((None, N_MOD, d), lambda i: (i // per_seq, 0, 0)),
                  _const_spec((1, d)),
                  _const_spec(win.shape),
                  pl.BlockSpec((tm, LANES), tab), pl.BlockSpec((tm, LANES), tab),
                  pl.BlockSpec((tm, LANES), tab), pl.BlockSpec((tm, LANES), tab),
                  _const_spec((1, LANES)), _const_spec((1, LANES)),
                  _const_spec((1, MLA_Q_RANK)), _const_spec((1, MLA_KV_RANK)),
                  _const_spec(wuq.shape), _const_spec(wukv.shape)],
        out_specs=[hq(LANES), hq(LANES), hq(LANES), vt, hq(MLA_QK_W), hq(MLA_QK_W), vt],
        out_shape=[sd((batch, DA_HEADS, seq, LANES), BF16), sd((batch, DA_HEADS, seq, LANES), BF16),
                   sd((batch, DA_HEADS, seq, LANES), BF16), sd((batch, DA_HEADS, LANES, seq), BF16),
                   sd((batch, MLA_HEADS, seq, MLA_QK_W), BF16), sd((batch, MLA_HEADS, seq, MLA_QK_W), BF16),
                   sd((batch, MLA_HEADS, LANES, seq), BF16)],
        compiler_params=_params("arbitrary"),
        name="inproj",
    )(x, mod, g, win, *tabs, *masks, gq, gkv, wuq, wukv)


class _Stream:
    def __init__(self, q_ref, q_mask, bias_lane, k_ref, vt_ref, acc_ref, sum_ref, qz_ref, k2max_ref, bufs, tk):
        self.q_ref, self.q_mask, self.bias_lane, self.k_ref, self.vt_ref = q_ref, q_mask, bias_lane, k_ref, vt_ref
        self.acc_ref, self.sum_ref, self.qz_ref, self.k2max_ref, self.bufs, self.tk = (
            acc_ref, sum_ref, qz_ref, k2max_ref, bufs, tk)

    def _keys(self, blk):
        return pl.ds(pl.multiple_of(blk * self.tk, self.tk), self.tk)

    def masked_q(self):
        q = self.q_ref[...].astype(F32)
        return q if self.q_mask is None else q * self.q_mask

    def store_key_norm(self, nblk):
        ones = jnp.ones((8, self.k_ref.shape[1]), BF16)
        unroll = KEY_NORM_UNROLL if nblk % KEY_NORM_UNROLL == 0 else 1

        def group(g, best):
            for u in range(unroll):
                k = self.k_ref[self._keys(g * unroll + u), :]
                norms = lax.dot_general(ones, k * k, (((1,), (1,)), ((), ())), preferred_element_type=F32)
                best = jnp.maximum(best, norms)
            return best
        best = lax.fori_loop(0, nblk // unroll, group, jnp.zeros((8, self.tk), F32))
        self.k2max_ref[...] = jnp.broadcast_to(jnp.max(best[0:1, :], axis=1, keepdims=True), self.k2max_ref.shape)

    def score_bound(self):
        q = self.masked_q()
        q2 = jnp.max(jnp.sum(q * q, axis=1, keepdims=True), axis=0, keepdims=True)
        return jnp.sqrt(q2 * self.k2max_ref[0:1, 0:1]) * 1.01 + 0.01

    def set_query(self, shift):
        q = self.masked_q()
        if shift is not None:
            lane = lax.broadcasted_iota(jnp.int32, q.shape, 1)
            q = jnp.where(lane == self.bias_lane, -shift, q)
        self.qz_ref[...] = q.astype(BF16)

    def _qk(self, blk):
        return lax.dot_general(self.k_ref[self._keys(blk), :], self.qz_ref[...], (((1,), (1,)), ((), ())),
                               preferred_element_type=F32)

    def _pv(self, blk, p):
        return jnp.dot(self.vt_ref[:, self._keys(blk)], p.astype(BF16), preferred_element_type=F32)

    def scores(self, blk, slot):
        s = self._qk(blk)
        self.bufs[slot][...] = s
        return jnp.max(s, axis=0, keepdims=True)

    def consume(self, blk, slot, blk_max, m, l):
        m_new = jnp.maximum(m, blk_max)
        p = jnp.exp2(self.bufs[slot][...] - m_new)
        alpha = jnp.exp2(m - m_new)
        l_new = alpha * l + jnp.sum(p, axis=0, keepdims=True)
        self.acc_ref[...] = alpha * self.acc_ref[...] + self._pv(blk, p)
        return m_new, l_new

    def accumulate(self, blk, part):
        p = jnp.exp2(self._qk(blk))
        self.acc_ref[...] += self._pv(blk, p)
        return part + jnp.sum(p.reshape(self.tk // 8, 8, p.shape[1]), axis=0)


def _flash_exact(streams, nblk, tq):
    m0 = jnp.full((1, tq), NEG_BIG, F32)
    l0 = jnp.zeros((1, tq), F32)
    for st in streams:
        st.set_query(None)
    first = tuple((m0, l0, st.scores(0, 0)) for st in streams)

    def pair(j, carry):
        b0 = 2 * j
        b2 = jnp.minimum(b0 + 2, nblk - 1)
        out = []
        for st, (m, l, max0) in zip(streams, carry):
            max1 = st.scores(b0 + 1, 1)
            m, l = st.consume(b0, 0, max0, m, l)
            max2 = st.scores(b2, 0)
            m, l = st.consume(b0 + 1, 1, max1, m, l)
            out.append((m, l, max2))
        return tuple(out)

    final = lax.fori_loop(0, nblk // 2, pair, first)
    for st, (_, l, _) in zip(streams, final):
        st.sum_ref[...] = l


def _flash_shifted(streams, bounds, nblk, tq):
    for st, bound in zip(streams, bounds):
        st.set_query(bound)

    unroll = SHIFT_UNROLL if nblk % SHIFT_UNROLL == 0 else 2

    def group(j, parts):
        for u in range(unroll):
            parts = tuple(st.accumulate(unroll * j + u, part) for st, part in zip(streams, parts))
        return parts

    parts = lax.fori_loop(0, nblk // unroll, group, tuple(jnp.zeros((8, tq), F32) for _ in streams))
    for st, part in zip(streams, parts):
        st.sum_ref[...] = jnp.sum(part, axis=0, keepdims=True)


def _flash(streams, nblk, tq, new_keys):
    assert nblk % 2 == 0

    @pl.when(new_keys)
    def _():
        for st in {id(st.k2max_ref): st for st in streams}.values():
            st.store_key_norm(nblk)

    for st in streams:
        st.acc_ref[...] = jnp.zeros(st.acc_ref.shape, F32)
    bounds = [st.score_bound() for st in streams]
    shift_ok = functools.reduce(jnp.maximum, bounds)[0, 0] <= SHIFT_MAX_BOUND

    @pl.when(shift_ok)
    def _():
        _flash_shifted(streams, bounds, nblk, tq)

    @pl.when(jnp.logical_not(shift_ok))
    def _():
        _flash_exact(streams, nblk, tq)


def _da_body(lam_ref, g_ref, m1_ref, m2_ref, q_ref, k1_ref, k2_ref, vt_ref, o_ref,
             acc_ref, sum_ref, qz_ref, k2max_ref, *bufs, tk, lambda_init):
    tq = q_ref.shape[0] // ATT_QSUB
    masks = (m1_ref[...], m2_ref[...])
    lam_scr = bufs[-1]
    new_head = pl.program_id(2) == 0

    @pl.when(new_head)
    def _():
        lq1, lk1, lq2, lk2 = (lam_ref[r:r + 1, :] for r in range(4))
        lam = (jnp.exp(jnp.sum(lq1 * lk1, axis=-1, keepdims=True))
               - jnp.exp(jnp.sum(lq2 * lk2, axis=-1, keepdims=True)) + lambda_init)
        lam_scr[...] = jnp.broadcast_to(lam, lam_scr.shape)

    def query_block(sub, carry):
        rows = pl.ds(pl.multiple_of(sub * tq, tq), tq)
        streams = [_Stream(q_ref.at[rows, :], masks[j], DA_BIAS_LANES[j], (k1_ref, k2_ref)[j], vt_ref, acc_ref.at[j],
                           sum_ref.at[j], qz_ref.at[j], k2max_ref.at[j], bufs[2 * j:2 * j + 2], tk) for j in range(2)]
        _flash(streams, k1_ref.shape[0] // tk, tq, jnp.logical_and(new_head, sub == 0))
        lam = lam_scr[0:1, 0:1]
        ot = acc_ref[0] / sum_ref[0] - lam * (acc_ref[1] / sum_ref[1])
        ot = ot * lax.rsqrt(jnp.mean(ot * ot, axis=0, keepdims=True) + EPS)
        o_ref[rows, :] = (ot.T * g_ref[...] * (1.0 - lambda_init)).astype(o_ref.dtype)
        return carry

    lax.fori_loop(0, ATT_QSUB, query_block, 0)


def _mla_body(q_ref, k_ref, vt_ref, o_ref, acc_ref, sum_ref, qz_ref, k2max_ref, *bufs, tk):
    nq = acc_ref.shape[0]
    tq = q_ref.shape[0] // (nq * ATT_QSUB)
    key_norm = k2max_ref.at[0]
    new_head = pl.program_id(2) == 0

    def query_blocks(sub, carry):
        rows = [pl.ds(pl.multiple_of((sub * nq + i) * tq, tq), tq) for i in range(nq)]
        streams = [_Stream(q_ref.at[rows[i], :], None, MLA_BIAS_LANE, k_ref, vt_ref, acc_ref.at[i], sum_ref.at[i],
                           qz_ref.at[i], key_norm, bufs[2 * i:2 * i + 2], tk) for i in range(nq)]
        _flash(streams, k_ref.shape[0] // tk, tq, jnp.logical_and(new_head, sub == 0))
        for i in range(nq):
            o_ref[rows[i], :] = (acc_ref[i] / sum_ref[i]).T.astype(o_ref.dtype)
        return carry

    lax.fori_loop(0, ATT_QSUB, query_blocks, 0)


def _attn_specs(batch, heads, seq, qk_w, nq, tq_max, tk_max):
    tq = min(tq_max, seq // nq)
    tk = min(tk_max, seq)
    q_spec = pl.BlockSpec((None, None, nq * tq, qk_w), lambda b, h, i: (b, h, i, 0))
    k_spec = pl.BlockSpec((None, None, seq, qk_w), lambda b, h, i: (b, h, 0, 0))
    vt_spec = pl.BlockSpec((None, None, LANES, seq), lambda b, h, i: (b, h, 0, 0))
    o_spec = pl.BlockSpec((None, nq * tq, LANES), lambda b, h, i: (b, i, h))
    o_shape = jax.ShapeDtypeStruct((batch, seq, heads * LANES), BF16)
    return tq, tk, (batch, heads, seq // (nq * tq)), q_spec, k_spec, vt_spec, o_spec, o_shape


def _attn_scratch(streams, key_sets, tq, tk, qk_w):
    return ([pltpu.VMEM((streams, LANES, tq), F32),
             pltpu.VMEM((streams, 1, tq), F32),
             pltpu.VMEM((streams, tq, qk_w), BF16),
             pltpu.VMEM((key_sets, 8, LANES), F32)]
            + [pltpu.VMEM((tk, tq), F32)] * (2 * streams))


def _da_attn(lam_vecs, subln_g, masks, q, k1, k2, vt, *, lambda_init):
    batch, heads, seq, _ = q.shape
    tq, tk, grid, q_spec, k_spec, vt_spec, o_spec, o_shape = _attn_specs(
        batch, heads, seq, LANES, ATT_QSUB, DA_TQ, DA_TK)
    lane_row = pl.BlockSpec((1, LANES), lambda b, h, i: (0, 0))
    return pl.pallas_call(
        functools.partial(_da_body, tk=tk, lambda_init=lambda_init),
        grid=grid,
        in_specs=[pl.BlockSpec(lam_vecs.shape, lambda b, h, i: (0, 0)), lane_row, lane_row, lane_row,
                  q_spec, k_spec, k_spec, vt_spec],
        out_specs=o_spec,
        out_shape=o_shape,
        scratch_shapes=_attn_scratch(2, 2, tq, tk, LANES) + [pltpu.VMEM((8, LANES), F32)],
        compiler_params=_params("arbitrary", "arbitrary", "arbitrary"),
        name="da_attn",
    )(lam_vecs, subln_g, *masks, q, k1, k2, vt)


def _mla_attn(q, k, vt):
    batch, heads, seq, qk_w = q.shape
    nq = MLA_QSTREAMS
    tq, tk, grid, q_spec, k_spec, vt_spec, o_spec, o_shape = _attn_specs(
        batch, heads, seq, qk_w, nq * ATT_QSUB, MLA_TQ, MLA_TK)
    return pl.pallas_call(
        functools.partial(_mla_body, tk=tk),
        grid=grid,
        in_specs=[q_spec, k_spec, vt_spec],
        out_specs=o_spec,
        out_shape=o_shape,
        scratch_shapes=_attn_scratch(nq, 1, tq, tk, qk_w),
        compiler_params=_params("arbitrary", "arbitrary", "arbitrary"),
        name="mla_attn",
    )(q, k, vt)


def _outproj_body(x_ref, mod_ref, oda_ref, oml_ref, wo_ref, o_ref):
    split = oda_ref.shape[1]
    mix = (jnp.dot(oda_ref[...], wo_ref[0:split, :], preferred_element_type=F32)
           + jnp.dot(oml_ref[...], wo_ref[split:, :], preferred_element_type=F32))
    o_ref[...] = x_ref[...] + mod_ref[5:6, :] * mix


def _outproj(x, mod, o_da, o_mla, wo, *, seq):
    t, d = x.shape
    tm = min(OUT_TM, seq)
    per_seq = seq // tm
    row = lambda i: (i, 0)
    return pl.pallas_call(
        _outproj_body,
        grid=(t // tm,),
        in_specs=[pl.BlockSpec((tm, d), row),
                  pl.BlockSpec((None, N_MOD, d), lambda i: (i // per_seq, 0, 0)),
                  pl.BlockSpec((tm, o_da.shape[1]), row),
                  pl.BlockSpec((tm, o_mla.shape[1]), row),
                  _const_spec(wo.shape)],
        out_specs=pl.BlockSpec((tm, d), row),
        out_shape=jax.ShapeDtypeStruct((t, d), F32),
        compiler_params=_params("arbitrary"),
        name="outproj",
    )(x, mod, o_da, o_mla, wo)


def _da_lane_layout():
    src = np.zeros(LANES, np.int32)
    rot = np.full(LANES, -1, np.int32)
    sign = np.zeros(LANES, np.float32)
    mask = np.zeros((2, LANES), np.float32)
    half = DA_ROT // 2
    for j in range(2):
        base = j * DA_QK_DIM
        for i in range(half):
            src[j * half + i] = base + i
            rot[j * half + i] = i
            sign[j * half + i] = -1.0
            src[64 + j * half + i] = base + half + i
            rot[64 + j * half + i] = i
            sign[64 + j * half + i] = 1.0
            mask[j, j * half + i] = mask[j, 64 + j * half + i] = 1.0
        plain = DA_QK_DIM - DA_ROT
        start = DA_ROT + j * 64
        for i in range(plain):
            src[start + i] = base + DA_ROT + i
            mask[j, start + i] = 1.0
    return src, rot, sign, mask


def _mla_lane_layout():
    src = np.full(LANES, -1, np.int32)
    rot = np.full(LANES, -1, np.int32)
    sign = np.zeros(LANES, np.float32)
    half = MLA_ROPE // 2
    for i in range(half):
        src[i], rot[i], sign[i] = i, i, -1.0
        src[64 + i], rot[64 + i], sign[64 + i] = half + i, i, 1.0
    return src, rot, sign


def _take_cols(w, src):
    src = [int(v) for v in src]
    parts, start = [], 0
    for i in range(1, len(src) + 1):
        same_run = i < len(src) and ((src[i] < 0 and src[i - 1] < 0) or (src[i - 1] >= 0 and src[i] == src[i - 1] + 1))
        if not same_run:
            width = i - start
            parts.append(jnp.zeros((w.shape[0], width), w.dtype) if src[start] < 0
                         else w[:, src[start]:src[start] + width])
            start = i
    return jnp.concatenate(parts, axis=1)


def _rope_tables(seq, dim, rot, sign):
    inv = ROPE_THETA ** (-jnp.arange(0, dim, 2, dtype=F32) / dim)
    ang = jnp.arange(seq, dtype=F32)[:, None] * _take_cols(inv[None, :], rot)
    return jnp.cos(ang), jnp.sin(ang) * jnp.asarray(sign)


def _prep_layer(l, ffn1_w1, ffn1_w3, ffn1_w2, w_in, mla_w_uq, mla_w_ukv, w_o, ffn2_w1, ffn2_w3, ffn2_w2):
    da_src, _, _, _ = _da_lane_layout()
    ml_src, _, _ = _mla_lane_layout()
    head_src = np.concatenate([h * 2 * DA_QK_DIM + da_src for h in range(DA_HEADS)])
    o1, o2, o3 = DA_Q_W, 2 * DA_Q_W, 2 * DA_Q_W + DA_V_W
    o5 = o3 + MLA_Q_RANK + MLA_KV_RANK
    wi = w_in[l].astype(BF16)
    win = jnp.concatenate([_take_cols(wi[:, :o1], head_src),
                           _take_cols(wi[:, o1:o2], head_src),
                           wi[:, o2:o5],
                           _take_cols(wi[:, o5:], ml_src)], axis=1)
    per_q = MLA_NOPE + MLA_ROPE
    uq_src = np.concatenate([np.concatenate([h * per_q + np.arange(MLA_NOPE),
                                             np.where(ml_src >= 0, h * per_q + MLA_NOPE + ml_src, -1)])
                             for h in range(MLA_HEADS)])
    per_kv = MLA_NOPE + MLA_V
    ukv_src = np.concatenate([h * per_kv + np.arange(MLA_NOPE) for h in range(MLA_HEADS)]
                             + [h * per_kv + MLA_NOPE + np.arange(MLA_V) for h in range(MLA_HEADS)])
    return dict(
        f1=(ffn1_w1[l].astype(BF16), ffn1_w3[l].astype(BF16), ffn1_w2[l].astype(BF16)),
        f2=(ffn2_w1[l].astype(BF16), ffn2_w3[l].astype(BF16), ffn2_w2[l].astype(BF16)),
        win=win,
        wuq=_take_cols(mla_w_uq[l].astype(BF16), uq_src),
        wukv=_take_cols(mla_w_ukv[l].astype(BF16), ukv_src),
        wo=w_o[l].astype(BF16),
    )


def _encode_group(x, mods, prepped, small, final_norm_g, tabs):
    batch, seq, d = x.shape
    xt = x.reshape(batch * seq, d)
    da_mask = _da_lane_layout()[3]
    masks = (jnp.asarray(da_mask[0:1]), jnp.asarray(da_mask[1:2]))
    depth = len(prepped)
    row = lambda v: v.reshape(1, -1)
    for l in range(depth):
        p, s, mod = prepped[l], small[l], mods[l]
        lambda_init = 0.8 - 0.6 * math.exp(-0.3 * l)
        xt = _ffn(xt, mod, row(s["ffn1_norm"]), *p["f1"], row(s["ffn1_norm"]),
                  seq=seq, mod_base=0, final_norm=False)
        qda, k1, k2, vtda, qml, kml, vtml = _inproj(
            xt, mod, row(s["attn_norm"]), p["win"], tabs, masks, row(s["mla_q_norm"]), row(s["mla_kv_norm"]),
            p["wuq"], p["wukv"], batch=batch, seq=seq)
        o_da = _da_attn(s["lam_vecs"], row(s["da_subln"]), masks, qda, k1, k2, vtda, lambda_init=lambda_init)
        o_mla = _mla_attn(qml, kml, vtml)
        xt = _outproj(xt, mod, o_da.reshape(batch * seq, -1), o_mla.reshape(batch * seq, -1),
                      p["wo"], seq=seq)
        last = l == depth - 1
        xt = _ffn(xt, mod, row(s["ffn2_norm"]), *p["f2"], row(final_norm_g),
                  seq=seq, mod_base=6, final_norm=last)
    return xt.reshape(batch, seq, d)


def kernel(x_prompt, x_sample, c_prompt, c_sample, ffn1_norm, ffn1_w1, ffn1_w3, ffn1_w2, attn_norm, w_in,
           da_lambda_q1, da_lambda_k1, da_lambda_q2, da_lambda_k2, da_subln, mla_q_norm, mla_w_uq, mla_kv_norm,
           mla_w_ukv, w_o, ffn2_norm, ffn2_w1, ffn2_w3, ffn2_w2, w_ada, b_ada, final_norm):
    depth = w_in.shape[0]
    nb_p, nb_s = c_prompt.shape[0], c_sample.shape[0]
    c_all = jnp.concatenate([c_prompt, c_sample], axis=0)
    rows = -(-c_all.shape[0] // 8) * 8
    c_pad = jnp.pad(c_all, ((0, rows - c_all.shape[0]), (0, 0)))

    prepped, small, mods_p, mods_s = [], [], [], []
    for l in range(depth):
        prepped.append(_prep_layer(l, ffn1_w1, ffn1_w3, ffn1_w2, w_in, mla_w_uq, mla_w_ukv, w_o,
                                   ffn2_w1, ffn2_w3, ffn2_w2))
        small.append(dict(
            ffn1_norm=ffn1_norm[l], attn_norm=attn_norm[l], ffn2_norm=ffn2_norm[l], da_subln=da_subln[l],
            mla_q_norm=mla_q_norm[l], mla_kv_norm=mla_kv_norm[l],
            lam_vecs=jnp.stack([da_lambda_q1[l], da_lambda_k1[l], da_lambda_q2[l], da_lambda_k2[l]])))
        m = _ada(c_pad, w_ada[l], b_ada[l].reshape(1, -1))
        mods_p.append(m[:nb_p].reshape(nb_p, N_MOD, D_MODEL))
        mods_s.append(m[nb_p:nb_p + nb_s].reshape(nb_s, N_MOD, D_MODEL))

    _, da_rot, da_sign, _ = _da_lane_layout()
    _, ml_rot, ml_sign = _mla_lane_layout()
    max_seq = max(x_prompt.shape[1], x_sample.shape[1])
    tabs = _rope_tables(max_seq, DA_ROT, da_rot, da_sign) + _rope_tables(max_seq, MLA_ROPE, ml_rot, ml_sign)

    y_prompt = _encode_group(x_prompt, mods_p, prepped, small, final_norm, tabs)
    y_sample = _encode_group(x_sample, mods_s, prepped, small, final_norm, tabs)
    return (y_prompt, y_sample)
```

```python
import functools
import math

import jax
import jax.numpy as jnp
import numpy as np
from jax import lax
from jax.experimental import pallas as pl
from jax.experimental.pallas import tpu as pltpu

D_MODEL = 2048
D_FF = 5632
N_MOD = 9
DA_HEADS = 8
DA_QK_DIM = 64
DA_V_DIM = 128
DA_ROT = 16
DA_Q_W = DA_HEADS * 2 * DA_QK_DIM
DA_V_W = DA_HEADS * DA_V_DIM
MLA_HEADS = 8
MLA_NOPE = 128
MLA_ROPE = 64
MLA_V = 128
MLA_Q_RANK = 512
MLA_KV_RANK = 256
ROPE_THETA = 500000.0
EPS = 1e-6
LANES = 128
MLA_QK_W = 2 * LANES
VMEM_LIMIT = 56 * 1024 * 1024
FFN_VMEM_LIMIT = 60 * 1024 * 1024
NEG_BIG = -1e30
LOG2E = math.log2(math.e)
SHIFT_MAX_BOUND = 40.0
DA_BIAS_LANES = (8, 0)
MLA_BIAS_LANE = LANES + 32

FFN_TM, FFN_TF = 1024, 512
FFN_ROW_CHUNK = 256
PROJ_TM = 256
OUT_TM = 512
DA_TQ, DA_TK = 1024, 1024
MLA_TQ, MLA_TK = 1024, 2048
MLA_QSTREAMS = 1
ATT_QSUB = 2
SHIFT_UNROLL = 4
KEY_NORM_UNROLL = 4
ADA_TN = 1024

BF16 = jnp.bfloat16
F32 = jnp.float32


def _params(*sem, vmem=VMEM_LIMIT):
    return pltpu.CompilerParams(dimension_semantics=sem, vmem_limit_bytes=vmem)


def _rms(x, g):
    return x * lax.rsqrt(jnp.mean(x * x, axis=-1, keepdims=True) + EPS) * g


def _silu(x):
    return x / (1.0 + jnp.exp(-x))


def _const_spec(shape):
    zeros = (0,) * len(shape)
    return pl.BlockSpec(shape, lambda *_: zeros, pipeline_mode=pl.Buffered(1))


def _ada_body(c_ref, w_ref, b_ref, o_ref):
    a = _silu(c_ref[...]).astype(BF16)
    o_ref[...] = jnp.dot(a, w_ref[...].astype(BF16), preferred_element_type=F32) + b_ref[...]


def _ada(c_pad, w, b):
    rows, d = c_pad.shape
    n = w.shape[1]
    return pl.pallas_call(
        _ada_body,
        grid=(n // ADA_TN,),
        in_specs=[pl.BlockSpec((rows, d), lambda j: (0, 0)),
                  pl.BlockSpec((d, ADA_TN), lambda j: (0, j)),
                  pl.BlockSpec((1, ADA_TN), lambda j: (0, j))],
        out_specs=pl.BlockSpec((rows, ADA_TN), lambda j: (0, j)),
        out_shape=jax.ShapeDtypeStruct((rows, n), F32),
        compiler_params=_params("arbitrary"),
        name="ada",
    )(c_pad, w, b)


def _ffn_body(x_ref, mod_ref, g_ref, w1_ref, w3_ref, w2_ref, fg_ref, o_ref, h_ref, *, mod_base, final_norm):
    j = pl.program_id(1)
    last = pl.num_programs(1) - 1
    chunks = [pl.ds(r, FFN_ROW_CHUNK) for r in range(0, x_ref.shape[0], FFN_ROW_CHUNK)]

    def swiglu(h):
        a = jnp.dot(h, w1_ref[...], preferred_element_type=F32)
        b = jnp.dot(h, w3_ref[...], preferred_element_type=F32)
        return jnp.dot((_silu(a) * b).astype(BF16), w2_ref[...], preferred_element_type=F32)

    @pl.when(j == 0)
    def _():
        shift = mod_ref[mod_base:mod_base + 1, :]
        scale = mod_ref[mod_base + 1:mod_base + 2, :]
        for rows in chunks:
            h = (_rms(x_ref[rows, :], g_ref[...]) * (1.0 + scale) + shift).astype(BF16)
            h_ref[rows, :] = h
            o_ref[rows, :] = swiglu(h)

    @pl.when(jnp.logical_and(j > 0, j < last))
    def _():
        o_ref[...] += swiglu(h_ref[...])

    @pl.when(j == last)
    def _():
        gate = mod_ref[mod_base + 2:mod_base + 3, :]
        for rows in chunks:
            y = x_ref[rows, :] + 0.5 * gate * (o_ref[rows, :] + swiglu(h_ref[rows, :]))
            if final_norm:
                y = _rms(y, fg_ref[...])
            o_ref[rows, :] = y


def _ffn(x, mod, g, w1, w3, w2, fg, *, seq, mod_base, final_norm):
    t, d = x.shape
    f = w1.shape[1]
    tm = min(FFN_TM, seq)
    tf = FFN_TF
    assert f // tf >= 2, "the first and the last hidden-dim step must be different steps"
    per_seq = seq // tm
    return pl.pallas_call(
        functools.partial(_ffn_body, mod_base=mod_base, final_norm=final_norm),
        grid=(t // tm, f // tf),
        in_specs=[pl.BlockSpec((tm, d), lambda i, j: (i, 0)),
                  pl.BlockSpec((None, N_MOD, d), lambda i, j: (i // per_seq, 0, 0)),
                  pl.BlockSpec((1, d), lambda i, j: (0, 0)),
                  pl.BlockSpec((d, tf), lambda i, j: (0, j)),
                  pl.BlockSpec((d, tf), lambda i, j: (0, j)),
                  pl.BlockSpec((tf, d), lambda i, j: (j, 0)),
                  pl.BlockSpec((1, d), lambda i, j: (0, 0))],
        out_specs=pl.BlockSpec((tm, d), lambda i, j: (i, 0)),
        out_shape=jax.ShapeDtypeStruct((t, d), F32),
        scratch_shapes=[pltpu.VMEM((tm, d), BF16)],
        compiler_params=_params("arbitrary", "arbitrary", vmem=FFN_VMEM_LIMIT),
        name="ffn_final" if final_norm else "ffn",
    )(x, mod, g, w1, w3, w2, fg)


def _rope(t, cos, sin_signed):
    return t * cos + pltpu.roll(t, LANES // 2, axis=1) * sin_signed


def _inproj_body(x_ref, mod_ref, g_ref, win_ref, cda_ref, sda_ref, cml_ref, sml_ref, m1_ref, m2_ref,
                 gq_ref, gkv_ref, wuq_ref, wukv_ref,
                 qda_ref, k1_ref, k2_ref, vtda_ref, qml_ref, kml_ref, vtml_ref):
    shift = mod_ref[3:4, :]
    scale = mod_ref[4:5, :]
    h = (_rms(x_ref[...], g_ref[...]) * (1.0 + scale) + shift).astype(BF16)
    cda, sda = cda_ref[...], sda_ref[...]
    cml, sml = cml_ref[...], sml_ref[...]
    mask1, mask2 = m1_ref[...], m2_ref[...]
    da_scale = DA_QK_DIM ** -0.5 * LOG2E
    mla_scale = (MLA_NOPE + MLA_ROPE) ** -0.5 * LOG2E

    def tile(a, k):
        return a[:, k * LANES:(k + 1) * LANES]

    def one_hot(lane):
        return (lax.broadcasted_iota(jnp.int32, (1, LANES), 1) == lane).astype(F32)

    pq = jnp.dot(h, win_ref[:, 0:DA_Q_W], preferred_element_type=F32)
    for hd in range(DA_HEADS):
        qda_ref[hd] = (_rope(tile(pq, hd), cda, sda) * da_scale).astype(BF16)
    pk = jnp.dot(h, win_ref[:, DA_Q_W:2 * DA_Q_W], preferred_element_type=F32)
    for hd in range(DA_HEADS):
        r = _rope(tile(pk, hd), cda, sda)
        k1_ref[hd] = (r * mask1 + one_hot(DA_BIAS_LANES[0])).astype(BF16)
        k2_ref[hd] = (r * mask2 + one_hot(DA_BIAS_LANES[1])).astype(BF16)
    pv = jnp.dot(h, win_ref[:, 2 * DA_Q_W:2 * DA_Q_W + DA_V_W], preferred_element_type=F32)
    for hd in range(DA_HEADS):
        vtda_ref[hd] = tile(pv, hd).T.astype(BF16)

    o3 = 2 * DA_Q_W + DA_V_W
    pr = jnp.dot(h, win_ref[:, o3:], preferred_element_type=F32)
    cq = _rms(pr[:, 0:MLA_Q_RANK], gq_ref[...]).astype(BF16)
    ckv = _rms(pr[:, MLA_Q_RANK:MLA_Q_RANK + MLA_KV_RANK], gkv_ref[...]).astype(BF16)
    kpe = (_rope(pr[:, MLA_Q_RANK + MLA_KV_RANK:], cml, sml) + one_hot(MLA_BIAS_LANE - LANES)).astype(BF16)

    qm = jnp.dot(cq, wuq_ref[...], preferred_element_type=F32)
    for hd in range(MLA_HEADS):
        qml_ref[hd, :, 0:LANES] = (tile(qm, 2 * hd) * mla_scale).astype(BF16)
        qml_ref[hd, :, LANES:] = (_rope(tile(qm, 2 * hd + 1), cml, sml) * mla_scale).astype(BF16)
    kv = jnp.dot(ckv, wukv_ref[...], preferred_element_type=F32)
    for hd in range(MLA_HEADS):
        kml_ref[hd, :, 0:LANES] = tile(kv, hd).astype(BF16)
        kml_ref[hd, :, LANES:] = kpe
        vtml_ref[hd] = tile(kv, MLA_HEADS + hd).T.astype(BF16)


def _inproj(x, mod, g, win, tabs, masks, gq, gkv, wuq, wukv, *, batch, seq):
    t, d = x.shape
    tm = min(PROJ_TM, seq)
    per_seq = seq // tm
    row = lambda i: (i, 0)
    tab = lambda i: (i % per_seq, 0)
    head_rows = lambda i: (i // per_seq, 0, i % per_seq, 0)
    head_cols = lambda i: (i // per_seq, 0, 0, i % per_seq)
    hq = lambda w: pl.BlockSpec((None, DA_HEADS, tm, w), head_rows)
    vt = pl.BlockSpec((None, DA_HEADS, LANES, tm), head_cols)
    sd = jax.ShapeDtypeStruct
    return pl.pallas_call(
        _inproj_body,
        grid=(t // tm,),
        in_specs=[pl.BlockSpec((tm, d), row),
                  pl.BlockSpec((None, N_MOD, d), lambda i: (i // per_seq, 0, 0)),
                  _const_spec((1, d)),
                  _const_spec(win.shape),
                  pl.BlockSpec((tm, LANES), tab), pl.BlockSpec((tm, LANES), tab),
                  pl.BlockSpec((tm, LANES), tab), pl.BlockSpec((tm, LANES), tab),
                  _const_spec((1, LANES)), _const_spec((1, LANES)),
                  _const_spec((1, MLA_Q_RANK)), _const_spec((1, MLA_KV_RANK)),
                  _const_spec(wuq.shape), _const_spec(wukv.shape)],
        out_specs=[hq(LANES), hq(LANES), hq(LANES), vt, hq(MLA_QK_W), hq(MLA_QK_W), vt],
        out_shape=[sd((batch, DA_HEADS, seq, LANES), BF16), sd((batch, DA_HEADS, seq, LANES), BF16),
                   sd((batch, DA_HEADS, seq, LANES), BF16), sd((batch, DA_HEADS, LANES, seq), BF16),
                   sd((batch, MLA_HEADS, seq, MLA_QK_W), BF16), sd((batch, MLA_HEADS, seq, MLA_QK_W), BF16),
                   sd((batch, MLA_HEADS, LANES, seq), BF16)],
        compiler_params=_params("arbitrary"),
        name="inproj",
    )(x, mod, g, win, *tabs, *masks, gq, gkv, wuq, wukv)


def _row_sq_norms(x):
    sq = x * x
    sq = functools.reduce(jnp.add, [sq[:, t:t + LANES] for t in range(0, sq.shape[1], LANES)])
    return lax.dot_general(jnp.ones((8, LANES), BF16), sq, (((1,), (1,)), ((), ())), preferred_element_type=F32)


class _Stream:
    def __init__(self, q_ref, q_mask, bias_lane, k_ref, vt_ref, acc_ref, sum_ref, qz_ref, k2max_ref, bufs, tk):
        self.q_ref, self.q_mask, self.bias_lane, self.k_ref, self.vt_ref = q_ref, q_mask, bias_lane, k_ref, vt_ref
        self.acc_ref, self.sum_ref, self.qz_ref, self.k2max_ref, self.bufs, self.tk = (
            acc_ref, sum_ref, qz_ref, k2max_ref, bufs, tk)

    def _keys(self, blk):
        return pl.ds(pl.multiple_of(blk * self.tk, self.tk), self.tk)

    def masked_q(self):
        q = self.q_ref[...].astype(F32)
        return q if self.q_mask is None else q * self.q_mask

    def store_key_norm(self, nblk):
        unroll = KEY_NORM_UNROLL if nblk % KEY_NORM_UNROLL == 0 else 1

        def group(g, best):
            for u in range(unroll):
                best = jnp.maximum(best, _row_sq_norms(self.k_ref[self._keys(g * unroll + u), :]))
            return best
        best = lax.fori_loop(0, nblk // unroll, group, jnp.zeros((8, self.tk), F32))
        self.k2max_ref[...] = jnp.broadcast_to(jnp.max(best[0:1, :], axis=1, keepdims=True), self.k2max_ref.shape)

    def score_bound(self):
        q = self.masked_q()
        q2 = jnp.max(jnp.sum(q * q, axis=1, keepdims=True), axis=0, keepdims=True)
        return jnp.sqrt(q2 * self.k2max_ref[0:1, 0:1]) * 1.01 + 0.01

    def set_query(self, shift):
        q = self.masked_q()
        if shift is not None:
            lane = lax.broadcasted_iota(jnp.int32, q.shape, 1)
            q = jnp.where(lane == self.bias_lane, -shift, q)
        self.qz_ref[...] = q.astype(BF16)

    def _qk(self, blk):
        return lax.dot_general(self.k_ref[self._keys(blk), :], self.qz_ref[...], (((1,), (1,)), ((), ())),
                               preferred_element_type=F32)

    def _pv(self, blk, p):
        return jnp.dot(self.vt_ref[:, self._keys(blk)], p.astype(BF16), preferred_element_type=F32)

    def scores(self, blk, slot):
        s = self._qk(blk)
        self.bufs[slot][...] = s
        return jnp.max(s, axis=0, keepdims=True)

    def consume(self, blk, slot, blk_max, m, l):
        m_new = jnp.maximum(m, blk_max)
        p = jnp.exp2(self.bufs[slot][...] - m_new)
        alpha = jnp.exp2(m - m_new)
        l_new = alpha * l + jnp.sum(p, axis=0, keepdims=True)
        self.acc_ref[...] = alpha * self.acc_ref[...] + self._pv(blk, p)
        return m_new, l_new

    def accumulate(self, blk, part):
        p = jnp.exp2(self._qk(blk))
        self.acc_ref[...] += self._pv(blk, p)
        return part + jnp.sum(p.reshape(self.tk // 8, 8, p.shape[1]), axis=0)


def _flash_exact(streams, nblk, tq):
    m0 = jnp.full((1, tq), NEG_BIG, F32)
    l0 = jnp.zeros((1, tq), F32)
    for st in streams:
        st.set_query(None)
    first = tuple((m0, l0, st.scores(0, 0)) for st in streams)

    def pair(j, carry):
        b0 = 2 * j
        b2 = jnp.minimum(b0 + 2, nblk - 1)
        out = []
        for st, (m, l, max0) in zip(streams, carry):
            max1 = st.scores(b0 + 1, 1)
            m, l = st.consume(b0, 0, max0, m, l)
            max2 = st.scores(b2, 0)
            m, l = st.consume(b0 + 1, 1, max1, m, l)
            out.append((m, l, max2))
        return tuple(out)

    final = lax.fori_loop(0, nblk // 2, pair, first)
    for st, (_, l, _) in zip(streams, final):
        st.sum_ref[...] = l


def _flash_shifted(streams, bounds, nblk, tq):
    for st, bound in zip(streams, bounds):
        st.set_query(bound)

    unroll = SHIFT_UNROLL if nblk % SHIFT_UNROLL == 0 else 2

    def group(j, parts):
        for u in range(unroll):
            parts = tuple(st.accumulate(unroll * j + u, part) for st, part in zip(streams, parts))
        return parts

    parts = lax.fori_loop(0, nblk // unroll, group, tuple(jnp.zeros((8, tq), F32) for _ in streams))
    for st, part in zip(streams, parts):
        st.sum_ref[...] = jnp.sum(part, axis=0, keepdims=True)


def _flash(streams, nblk, tq, new_keys):
    assert nblk % 2 == 0

    @pl.when(new_keys)
    def _():
        for st in {id(st.k2max_ref): st for st in streams}.values():
            st.store_key_norm(nblk)

    for st in streams:
        st.acc_ref[...] = jnp.zeros(st.acc_ref.shape, F32)
    bounds = [st.score_bound() for st in streams]
    shift_ok = functools.reduce(jnp.maximum, bounds)[0, 0] <= SHIFT_MAX_BOUND

    @pl.when(shift_ok)
    def _():
        _flash_shifted(streams, bounds, nblk, tq)

    @pl.when(jnp.logical_not(shift_ok))
    def _():
        _flash_exact(streams, nblk, tq)


def _da_body(lam_ref, g_ref, m1_ref, m2_ref, q_ref, k1_ref, k2_ref, vt_ref, o_ref,
             acc_ref, sum_ref, qz_ref, k2max_ref, *bufs, tk, lambda_init):
    tq = q_ref.shape[0] // ATT_QSUB
    masks = (m1_ref[...], m2_ref[...])
    lam_scr = bufs[-1]
    new_head = pl.program_id(2) == 0

    @pl.when(new_head)
    def _():
        lq1, lk1, lq2, lk2 = (lam_ref[r:r + 1, :] for r in range(4))
        lam = (jnp.exp(jnp.sum(lq1 * lk1, axis=-1, keepdims=True))
               - jnp.exp(jnp.sum(lq2 * lk2, axis=-1, keepdims=True)) + lambda_init)
        lam_scr[...] = jnp.broadcast_to(lam, lam_scr.shape)

    def query_block(sub, carry):
        rows = pl.ds(pl.multiple_of(sub * tq, tq), tq)
        streams = [_Stream(q_ref.at[rows, :], masks[j], DA_BIAS_LANES[j], (k1_ref, k2_ref)[j], vt_ref, acc_ref.at[j],
                           sum_ref.at[j], qz_ref.at[j], k2max_ref.at[j], bufs[2 * j:2 * j + 2], tk) for j in range(2)]
        _flash(streams, k1_ref.shape[0] // tk, tq, jnp.logical_and(new_head, sub == 0))
        lam = lam_scr[0:1, 0:1]
        ot = acc_ref[0] / sum_ref[0] - lam * (acc_ref[1] / sum_ref[1])
        ot = ot * lax.rsqrt(jnp.mean(ot * ot, axis=0, keepdims=True) + EPS)
        o_ref[rows, :] = (ot.T * g_ref[...] * (1.0 - lambda_init)).astype(o_ref.dtype)
        return carry

    lax.fori_loop(0, ATT_QSUB, query_block, 0)


def _mla_body(q_ref, k_ref, vt_ref, o_ref, acc_ref, sum_ref, qz_ref, k2max_ref, *bufs, tk):
    nq = acc_ref.shape[0]
    tq = q_ref.shape[0] // (nq * ATT_QSUB)
    key_norm = k2max_ref.at[0]
    new_head = pl.program_id(2) == 0

    def query_blocks(sub, carry):
        rows = [pl.ds(pl.multiple_of((sub * nq + i) * tq, tq), tq) for i in range(nq)]
        streams = [_Stream(q_ref.at[rows[i], :], None, MLA_BIAS_LANE, k_ref, vt_ref, acc_ref.at[i], sum_ref.at[i],
                           qz_ref.at[i], key_norm, bufs[2 * i:2 * i + 2], tk) for i in range(nq)]
        _flash(streams, k_ref.shape[0] // tk, tq, jnp.logical_and(new_head, sub == 0))
        for i in range(nq):
            o_ref[rows[i], :] = (acc_ref[i] / sum_ref[i]).T.astype(o_ref.dtype)
        return carry

    lax.fori_loop(0, ATT_QSUB, query_blocks, 0)


def _attn_specs(batch, heads, seq, qk_w, nq, tq_max, tk_max):
    tq = min(tq_max, seq // nq)
    tk = min(tk_max, seq)
    q_spec = pl.BlockSpec((None, None, nq * tq, qk_w), lambda b, h, i: (b, h, i, 0))
    k_spec = pl.BlockSpec((None, None, seq, qk_w), lambda b, h, i: (b, h, 0, 0))
    vt_spec = pl.BlockSpec((None, None, LANES, seq), lambda b, h, i: (b, h, 0, 0))
    o_spec = pl.BlockSpec((None, nq * tq, LANES), lambda b, h, i: (b, i, h))
    o_shape = jax.ShapeDtypeStruct((batch, seq, heads * LANES), BF16)
    return tq, tk, (batch, heads, seq // (nq * tq)), q_spec, k_spec, vt_spec, o_spec, o_shape


def _attn_scratch(streams, key_sets, tq, tk, qk_w):
    return ([pltpu.VMEM((streams, LANES, tq), F32),
             pltpu.VMEM((streams, 1, tq), F32),
             pltpu.VMEM((streams, tq, qk_w), BF16),
             pltpu.VMEM((key_sets, 8, LANES), F32)]
            + [pltpu.VMEM((tk, tq), F32)] * (2 * streams))


def _da_attn(lam_vecs, subln_g, masks, q, k1, k2, vt, *, lambda_init):
    batch, heads, seq, _ = q.shape
    tq, tk, grid, q_spec, k_spec, vt_spec, o_spec, o_shape = _attn_specs(
        batch, heads, seq, LANES, ATT_QSUB, DA_TQ, DA_TK)
    lane_row = pl.BlockSpec((1, LANES), lambda b, h, i: (0, 0))
    return pl.pallas_call(
        functools.partial(_da_body, tk=tk, lambda_init=lambda_init),
        grid=grid,
        in_specs=[pl.BlockSpec(lam_vecs.shape, lambda b, h, i: (0, 0)), lane_row, lane_row, lane_row,
                  q_spec, k_spec, k_spec, vt_spec],
        out_specs=o_spec,
        out_shape=o_shape,
        scratch_shapes=_attn_scratch(2, 2, tq, tk, LANES) + [pltpu.VMEM((8, LANES), F32)],
        compiler_params=_params("arbitrary", "arbitrary", "arbitrary"),
        name="da_attn",
    )(lam_vecs, subln_g, *masks, q, k1, k2, vt)


def _mla_attn(q, k, vt):
    batch, heads, seq, qk_w = q.shape
    nq = MLA_QSTREAMS
    tq, tk, grid, q_spec, k_spec, vt_spec, o_spec, o_shape = _attn_specs(
        batch, heads, seq, qk_w, nq * ATT_QSUB, MLA_TQ, MLA_TK)
    return pl.pallas_call(
        functools.partial(_mla_body, tk=tk),
        grid=grid,
        in_specs=[q_spec, k_spec, vt_spec],
        out_specs=o_spec,
        out_shape=o_shape,
        scratch_shapes=_attn_scratch(nq, 1, tq, tk, qk_w),
        compiler_params=_params("arbitrary", "arbitrary", "arbitrary"),
        name="mla_attn",
    )(q, k, vt)


def _outproj_body(x_ref, mod_ref, oda_ref, oml_ref, wo_ref, o_ref):
    split = oda_ref.shape[1]
    mix = (jnp.dot(oda_ref[...], wo_ref[0:split, :], preferred_element_type=F32)
           + jnp.dot(oml_ref[...], wo_ref[split:, :], preferred_element_type=F32))
    o_ref[...] = x_ref[...] + mod_ref[5:6, :] * mix


def _outproj(x, mod, o_da, o_mla, wo, *, seq):
    t, d = x.shape
    tm = min(OUT_TM, seq)
    per_seq = seq // tm
    row = lambda i: (i, 0)
    return pl.pallas_call(
        _outproj_body,
        grid=(t // tm,),
        in_specs=[pl.BlockSpec((tm, d), row),
                  pl.BlockSpec((None, N_MOD, d), lambda i: (i // per_seq, 0, 0)),
                  pl.BlockSpec((tm, o_da.shape[1]), row),
                  pl.BlockSpec((tm, o_mla.shape[1]), row),
                  _const_spec(wo.shape)],
        out_specs=pl.BlockSpec((tm, d), row),
        out_shape=jax.ShapeDtypeStruct((t, d), F32),
        compiler_params=_params("arbitrary"),
        name="outproj",
    )(x, mod, o_da, o_mla, wo)


def _da_lane_layout():
    src = np.zeros(LANES, np.int32)
    rot = np.full(LANES, -1, np.int32)
    sign = np.zeros(LANES, np.float32)
    mask = np.zeros((2, LANES), np.float32)
    half = DA_ROT // 2
    for j in range(2):
        base = j * DA_QK_DIM
        for i in range(half):
            src[j * half + i] = base + i
            rot[j * half + i] = i
            sign[j * half + i] = -1.0
            src[64 + j * half + i] = base + half + i
            rot[64 + j * half + i] = i
            sign[64 + j * half + i] = 1.0
            mask[j, j * half + i] = mask[j, 64 + j * half + i] = 1.0
        plain = DA_QK_DIM - DA_ROT
        start = DA_ROT + j * 64
        for i in range(plain):
            src[start + i] = base + DA_ROT + i
            mask[j, start + i] = 1.0
    return src, rot, sign, mask


def _mla_lane_layout():
    src = np.full(LANES, -1, np.int32)
    rot = np.full(LANES, -1, np.int32)
    sign = np.zeros(LANES, np.float32)
    half = MLA_ROPE // 2
    for i in range(half):
        src[i], rot[i], sign[i] = i, i, -1.0
        src[64 + i], rot[64 + i], sign[64 + i] = half + i, i, 1.0
    return src, rot, sign


def _take_cols(w, src):
    src = [int(v) for v in src]
    parts, start = [], 0
    for i in range(1, len(src) + 1):
        same_run = i < len(src) and ((src[i] < 0 and src[i - 1] < 0) or (src[i - 1] >= 0 and src[i] == src[i - 1] + 1))
        if not same_run:
            width = i - start
            parts.append(jnp.zeros((w.shape[0], width), w.dtype) if src[start] < 0
                         else w[:, src[start]:src[start] + width])
            start = i
    return jnp.concatenate(parts, axis=1)


def _rope_tables(seq, dim, rot, sign):
    inv = ROPE_THETA ** (-jnp.arange(0, dim, 2, dtype=F32) / dim)
    ang = jnp.arange(seq, dtype=F32)[:, None] * _take_cols(inv[None, :], rot)
    return jnp.cos(ang), jnp.sin(ang) * jnp.asarray(sign)


def _prep_layer(l, ffn1_w1, ffn1_w3, ffn1_w2, w_in, mla_w_uq, mla_w_ukv, w_o, ffn2_w1, ffn2_w3, ffn2_w2):
    da_src, _, _, _ = _da_lane_layout()
    ml_src, _, _ = _mla_lane_layout()
    head_src = np.concatenate([h * 2 * DA_QK_DIM + da_src for h in range(DA_HEADS)])
    o1, o2, o3 = DA_Q_W, 2 * DA_Q_W, 2 * DA_Q_W + DA_V_W
    o5 = o3 + MLA_Q_RANK + MLA_KV_RANK
    wi = w_in[l].astype(BF16)
    win = jnp.concatenate([_take_cols(wi[:, :o1], head_src),
                           _take_cols(wi[:, o1:o2], head_src),
                           wi[:, o2:o5],
                           _take_cols(wi[:, o5:], ml_src)], axis=1)
    per_q = MLA_NOPE + MLA_ROPE
    uq_src = np.concatenate([np.concatenate([h * per_q + np.arange(MLA_NOPE),
                                             np.where(ml_src >= 0, h * per_q + MLA_NOPE + ml_src, -1)])
                             for h in range(MLA_HEADS)])
    per_kv = MLA_NOPE + MLA_V
    ukv_src = np.concatenate([h * per_kv + np.arange(MLA_NOPE) for h in range(MLA_HEADS)]
                             + [h * per_kv + MLA_NOPE + np.arange(MLA_V) for h in range(MLA_HEADS)])
    return dict(
        f1=(ffn1_w1[l].astype(BF16), ffn1_w3[l].astype(BF16), ffn1_w2[l].astype(BF16)),
        f2=(ffn2_w1[l].astype(BF16), ffn2_w3[l].astype(BF16), ffn2_w2[l].astype(BF16)),
        win=win,
        wuq=_take_cols(mla_w_uq[l].astype(BF16), uq_src),
        wukv=_take_cols(mla_w_ukv[l].astype(BF16), ukv_src),
        wo=w_o[l].astype(BF16),
    )


def _encode_group(x, mods, prepped, small, final_norm_g, tabs):
    batch, seq, d = x.shape
    xt = x.reshape(batch * seq, d)
    da_mask = _da_lane_layout()[3]
    masks = (jnp.asarray(da_mask[0:1]), jnp.asarray(da_mask[1:2]))
    depth = len(prepped)
    row = lambda v: v.reshape(1, -1)
    for l in range(depth):
        p, s, mod = prepped[l], small[l], mods[l]
        lambda_init = 0.8 - 0.6 * math.exp(-0.3 * l)
        xt = _ffn(xt, mod, row(s["ffn1_norm"]), *p["f1"], row(s["ffn1_norm"]),
                  seq=seq, mod_base=0, final_norm=False)
        qda, k1, k2, vtda, qml, kml, vtml = _inproj(
            xt, mod, row(s["attn_norm"]), p["win"], tabs, masks, row(s["mla_q_norm"]), row(s["mla_kv_norm"]),
            p["wuq"], p["wukv"], batch=batch, seq=seq)
        o_da = _da_attn(s["lam_vecs"], row(s["da_subln"]), masks, qda, k1, k2, vtda, lambda_init=lambda_init)
        o_mla = _mla_attn(qml, kml, vtml)
        xt = _outproj(xt, mod, o_da.reshape(batch * seq, -1), o_mla.reshape(batch * seq, -1),
                      p["wo"], seq=seq)
        last = l == depth - 1
        xt = _ffn(xt, mod, row(s["ffn2_norm"]), *p["f2"], row(final_norm_g),
                  seq=seq, mod_base=6, final_norm=last)
    return xt.reshape(batch, seq, d)


def kernel(x_prompt, x_sample, c_prompt, c_sample, ffn1_norm, ffn1_w1, ffn1_w3, ffn1_w2, attn_norm, w_in,
           da_lambda_q1, da_lambda_k1, da_lambda_q2, da_lambda_k2, da_subln, mla_q_norm, mla_w_uq, mla_kv_norm,
           mla_w_ukv, w_o, ffn2_norm, ffn2_w1, ffn2_w3, ffn2_w2, w_ada, b_ada, final_norm):
    depth = w_in.shape[0]
    nb_p, nb_s = c_prompt.shape[0], c_sample.shape[0]
    c_all = jnp.concatenate([c_prompt, c_sample], axis=0)
    rows = -(-c_all.shape[0] // 8) * 8
    c_pad = jnp.pad(c_all, ((0, rows - c_all.shape[0]), (0, 0)))

    prepped, small, mods_p, mods_s = [], [], [], []
    for l in range(depth):
        prepped.append(_prep_layer(l, ffn1_w1, ffn1_w3, ffn1_w2, w_in, mla_w_uq, mla_w_ukv, w_o,
                                   ffn2_w1, ffn2_w3, ffn2_w2))
        small.append(dict(
            ffn1_norm=ffn1_norm[l], attn_norm=attn_norm[l], ffn2_norm=ffn2_norm[l], da_subln=da_subln[l],
            mla_q_norm=mla_q_norm[l], mla_kv_norm=mla_kv_norm[l],
            lam_vecs=jnp.stack([da_lambda_q1[l], da_lambda_k1[l], da_lambda_q2[l], da_lambda_k2[l]])))
        m = _ada(c_pad, w_ada[l], b_ada[l].reshape(1, -1))
        mods_p.append(m[:nb_p].reshape(nb_p, N_MOD, D_MODEL))
        mods_s.append(m[nb_p:nb_p + nb_s].reshape(nb_s, N_MOD, D_MODEL))

    _, da_rot, da_sign, _ = _da_lane_layout()
    _, ml_rot, ml_sign = _mla_lane_layout()
    max_seq = max(x_prompt.shape[1], x_sample.shape[1])
    tabs = _rope_tables(max_seq, DA_ROT, da_rot, da_sign) + _rope_tables(max_seq, MLA_ROPE, ml_rot, ml_sign)

    y_prompt = _encode_group(x_prompt, mods_p, prepped, small, final_norm, tabs)
    y_sample = _encode_group(x_sample, mods_s, prepped, small, final_norm, tabs)
    return (y_prompt, y_sample)
```

```python
import functools
import math

import jax
import jax.numpy as jnp
import numpy as np
from jax import lax
from jax.experimental import pallas as pl
from jax.experimental.pallas import tpu as pltpu

D_MODEL = 2048
D_FF = 5632
N_MOD = 9
DA_HEADS = 8
DA_QK_DIM = 64
DA_V_DIM = 128
DA_ROT = 16
DA_Q_W = DA_HEADS * 2 * DA_QK_DIM
DA_V_W = DA_HEADS * DA_V_DIM
MLA_HEADS = 8
MLA_NOPE = 128
MLA_ROPE = 64
MLA_V = 128
MLA_Q_RANK = 512
MLA_KV_RANK = 256
ROPE_THETA = 500000.0
EPS = 1e-6
LANES = 128
MLA_QK_W = 2 * LANES
VMEM_LIMIT = 56 * 1024 * 1024
FFN_VMEM_LIMIT = 60 * 1024 * 1024
NEG_BIG = -1e30
LOG2E = math.log2(math.e)
SHIFT_MAX_BOUND = 40.0
DA_BIAS_LANES = (8, 0)
MLA_BIAS_LANE = LANES + 32

FFN_TM, FFN_TF = 1024, 512
FFN_ROW_CHUNK = 512
PROJ_TM = 256
OUT_TM = 512
DA_TQ, DA_TK = 1024, 1024
MLA_TQ, MLA_TK = 1024, 2048
MLA_QSTREAMS = 1
ATT_QSUB = 2
SHIFT_UNROLL = 4
KEY_NORM_UNROLL = 4
ADA_TN = 1024

BF16 = jnp.bfloat16
F32 = jnp.float32


def _params(*sem, vmem=VMEM_LIMIT):
    return pltpu.CompilerParams(dimension_semantics=sem, vmem_limit_bytes=vmem)


def _rms(x, g):
    return x * lax.rsqrt(jnp.mean(x * x, axis=-1, keepdims=True) + EPS) * g


def _silu(x):
    return x / (1.0 + jnp.exp(-x))


def _const_spec(shape):
    zeros = (0,) * len(shape)
    return pl.BlockSpec(shape, lambda *_: zeros, pipeline_mode=pl.Buffered(1))


def _ada_body(c_ref, w_ref, b_ref, o_ref):
    a = _silu(c_ref[...]).astype(BF16)
    o_ref[...] = jnp.dot(a, w_ref[...].astype(BF16), preferred_element_type=F32) + b_ref[...]


def _ada(c_pad, w, b):
    rows, d = c_pad.shape
    n = w.shape[1]
    return pl.pallas_call(
        _ada_body,
        grid=(n // ADA_TN,),
        in_specs=[pl.BlockSpec((rows, d), lambda j: (0, 0)),
                  pl.BlockSpec((d, ADA_TN), lambda j: (0, j)),
                  pl.BlockSpec((1, ADA_TN), lambda j: (0, j))],
        out_specs=pl.BlockSpec((rows, ADA_TN), lambda j: (0, j)),
        out_shape=jax.ShapeDtypeStruct((rows, n), F32),
        compiler_params=_params("arbitrary"),
        name="ada",
    )(c_pad, w, b)


def _ffn_body(x_ref, mod_ref, g_ref, w1_ref, w3_ref, w2_ref, fg_ref, o_ref, h_ref, *, mod_base, final_norm):
    j = pl.program_id(1)
    last = pl.num_programs(1) - 1
    chunks = [pl.ds(r, FFN_ROW_CHUNK) for r in range(0, x_ref.shape[0], FFN_ROW_CHUNK)]

    def swiglu(h):
        a = jnp.dot(h, w1_ref[...], preferred_element_type=F32)
        b = jnp.dot(h, w3_ref[...], preferred_element_type=F32)
        return jnp.dot((_silu(a) * b).astype(BF16), w2_ref[...], preferred_element_type=F32)

    @pl.when(j == 0)
    def _():
        shift = mod_ref[mod_base:mod_base + 1, :]
        scale = mod_ref[mod_base + 1:mod_base + 2, :]
        for rows in chunks:
            h = (_rms(x_ref[rows, :], g_ref[...]) * (1.0 + scale) + shift).astype(BF16)
            h_ref[rows, :] = h
            o_ref[rows, :] = swiglu(h)

    @pl.when(jnp.logical_and(j > 0, j < last))
    def _():
        o_ref[...] += swiglu(h_ref[...])

    @pl.when(j == last)
    def _():
        gate = mod_ref[mod_base + 2:mod_base + 3, :]
        for rows in chunks:
            y = x_ref[rows, :] + 0.5 * gate * (o_ref[rows, :] + swiglu(h_ref[rows, :]))
            if final_norm:
                y = _rms(y, fg_ref[...])
            o_ref[rows, :] = y


def _ffn(x, mod, g, w1, w3, w2, fg, *, seq, mod_base, final_norm):
    t, d = x.shape
    f = w1.shape[1]
    tm = min(FFN_TM, seq)
    tf = FFN_TF
    assert f // tf >= 2, "the first and the last hidden-dim step must be different steps"
    per_seq = seq // tm
    return pl.pallas_call(
        functools.partial(_ffn_body, mod_base=mod_base, final_norm=final_norm),
        grid=(t // tm, f // tf),
        in_specs=[pl.BlockSpec((tm, d), lambda i, j: (i, 0)),
                  pl.BlockSpec((None, N_MOD, d), lambda i, j: (i // per_seq, 0, 0)),
                  pl.BlockSpec((1, d), lambda i, j: (0, 0)),
                  pl.BlockSpec((d, tf), lambda i, j: (0, j)),
                  pl.BlockSpec((d, tf), lambda i, j: (0, j)),
                  pl.BlockSpec((tf, d), lambda i, j: (j, 0)),
                  pl.BlockSpec((1, d), lambda i, j: (0, 0))],
        out_specs=pl.BlockSpec((tm, d), lambda i, j: (i, 0)),
        out_shape=jax.ShapeDtypeStruct((t, d), F32),
        scratch_shapes=[pltpu.VMEM((tm, d), BF16)],
        compiler_params=_params("arbitrary", "arbitrary", vmem=FFN_VMEM_LIMIT),
        name="ffn_final" if final_norm else "ffn",
    )(x, mod, g, w1, w3, w2, fg)


def _rope(t, cos, sin_signed):
    return t * cos + pltpu.roll(t, LANES // 2, axis=1) * sin_signed


def _inproj_body(x_ref, mod_ref, g_ref, win_ref, cda_ref, sda_ref, cml_ref, sml_ref, m1_ref, m2_ref,
                 gq_ref, gkv_ref, wuq_ref, wukv_ref,
                 qda_ref, k1_ref, k2_ref, vtda_ref, qml_ref, kml_ref, vtml_ref):
    shift = mod_ref[3:4, :]
    scale = mod_ref[4:5, :]
    h = (_rms(x_ref[...], g_ref[...]) * (1.0 + scale) + shift).astype(BF16)
    cda, sda = cda_ref[...], sda_ref[...]
    cml, sml = cml_ref[...], sml_ref[...]
    mask1, mask2 = m1_ref[...], m2_ref[...]
    da_scale = DA_QK_DIM ** -0.5 * LOG2E
    mla_scale = (MLA_NOPE + MLA_ROPE) ** -0.5 * LOG2E

    def tile(a, k):
        return a[:, k * LANES:(k + 1) * LANES]

    def one_hot(lane):
        return (lax.broadcasted_iota(jnp.int32, (1, LANES), 1) == lane).astype(F32)

    pq = jnp.dot(h, win_ref[:, 0:DA_Q_W], preferred_element_type=F32)
    for hd in range(DA_HEADS):
        qda_ref[hd] = (_rope(tile(pq, hd), cda, sda) * da_scale).astype(BF16)
    pk = jnp.dot(h, win_ref[:, DA_Q_W:2 * DA_Q_W], preferred_element_type=F32)
    for hd in range(DA_HEADS):
        r = _rope(tile(pk, hd), cda, sda)
        k1_ref[hd] = (r * mask1 + one_hot(DA_BIAS_LANES[0])).astype(BF16)
        k2_ref[hd] = (r * mask2 + one_hot(DA_BIAS_LANES[1])).astype(BF16)
    pv = jnp.dot(h, win_ref[:, 2 * DA_Q_W:2 * DA_Q_W + DA_V_W], preferred_element_type=F32)
    for hd in range(DA_HEADS):
        vtda_ref[hd] = tile(pv, hd).T.astype(BF16)

    o3 = 2 * DA_Q_W + DA_V_W
    pr = jnp.dot(h, win_ref[:, o3:], preferred_element_type=F32)
    cq = _rms(pr[:, 0:MLA_Q_RANK], gq_ref[...]).astype(BF16)
    ckv = _rms(pr[:, MLA_Q_RANK:MLA_Q_RANK + MLA_KV_RANK], gkv_ref[...]).astype(BF16)
    kpe = (_rope(pr[:, MLA_Q_RANK + MLA_KV_RANK:], cml, sml) + one_hot(MLA_BIAS_LANE - LANES)).astype(BF16)

    qm = jnp.dot(cq, wuq_ref[...], preferred_element_type=F32)
    for hd in range(MLA_HEADS):
        qml_ref[hd, :, 0:LANES] = (tile(qm, 2 * hd) * mla_scale).astype(BF16)
        qml_ref[hd, :, LANES:] = (_rope(tile(qm, 2 * hd + 1), cml, sml) * mla_scale).astype(BF16)
    kv = jnp.dot(ckv, wukv_ref[...], preferred_element_type=F32)
    for hd in range(MLA_HEADS):
        kml_ref[hd, :, 0:LANES] = tile(kv, hd).astype(BF16)
        kml_ref[hd, :, LANES:] = kpe
        vtml_ref[hd] = tile(kv, MLA_HEADS + hd).T.astype(BF16)


def _inproj(x, mod, g, win, tabs, masks, gq, gkv, wuq, wukv, *, batch, seq):
    t, d = x.shape
    tm = min(PROJ_TM, seq)
    per_seq = seq // tm
    row = lambda i: (i, 0)
    tab = lambda i: (i % per_seq, 0)
    head_rows = lambda i: (i // per_seq, 0, i % per_seq, 0)
    head_cols = lambda i: (i // per_seq, 0, 0, i % per_seq)
    hq = lambda w: pl.BlockSpec((None, DA_HEADS, tm, w), head_rows)
    vt = pl.BlockSpec((None, DA_HEADS, LANES, tm), head_cols)
    sd = jax.ShapeDtypeStruct
    return pl.pallas_call(
        _inproj_body,
        grid=(t // tm,),
        in_specs=[pl.BlockSpec((tm, d), row),
                  pl.BlockSpec((None, N_MOD, d), lambda i: (i // per_seq, 0, 0)),
                  _const_spec((1, d)),
                  _const_spec(win.shape),
                  pl.BlockSpec((tm, LANES), tab), pl.BlockSpec((tm, LANES), tab),
                  pl.BlockSpec((tm, LANES), tab), pl.BlockSpec((tm, LANES), tab),
                  _const_spec((1, LANES)), _const_spec((1, LANES)),
                  _const_spec((1, MLA_Q_RANK)), _const_spec((1, MLA_KV_RANK)),
                  _const_spec(wuq.shape), _const_spec(wukv.shape)],
        out_specs=[hq(LANES), hq(LANES), hq(LANES), vt, hq(MLA_QK_W), hq(MLA_QK_W), vt],
        out_shape=[sd((batch, DA_HEADS, seq, LANES), BF16), sd((batch, DA_HEADS, seq, LANES), BF16),
                   sd((batch, DA_HEADS, seq, LANES), BF16), sd((batch, DA_HEADS, LANES, seq), BF16),
                   sd((batch, MLA_HEADS, seq, MLA_QK_W), BF16), sd((batch, MLA_HEADS, seq, MLA_QK_W), BF16),
                   sd((batch, MLA_HEADS, LANES, seq), BF16)],
        compiler_params=_params("arbitrary"),
        name="inproj",
    )(x, mod, g, win, *tabs, *masks, gq, gkv, wuq, wukv)


def _row_sq_norms(x):
    sq = x * x
    sq = functools.reduce(jnp.add, [sq[:, t:t + LANES] for t in range(0, sq.shape[1], LANES)])
    return lax.dot_general(jnp.ones((8, LANES), BF16), sq, (((1,), (1,)), ((), ())), preferred_element_type=F32)


class _Stream:
    def __init__(self, q_ref, q_mask, bias_lane, k_ref, vt_ref, acc_ref, sum_ref, qz_ref, k2max_ref, bufs, tk):
        self.q_ref, self.q_mask, self.bias_lane, self.k_ref, self.vt_ref = q_ref, q_mask, bias_lane, k_ref, vt_ref
        self.acc_ref, self.sum_ref, self.qz_ref, self.k2max_ref, self.bufs, self.tk = (
            acc_ref, sum_ref, qz_ref, k2max_ref, bufs, tk)

    def _keys(self, blk):
        return pl.ds(pl.multiple_of(blk * self.tk, self.tk), self.tk)

    def masked_q(self):
        q = self.q_ref[...].astype(F32)
        return q if self.q_mask is None else q * self.q_mask

    def store_key_norm(self, nblk):
        unroll = KEY_NORM_UNROLL if nblk % KEY_NORM_UNROLL == 0 else 1

        def group(g, best):
            for u in range(unroll):
                best = jnp.maximum(best, _row_sq_norms(self.k_ref[self._keys(g * unroll + u), :]))
            return best
        best = lax.fori_loop(0, nblk // unroll, group, jnp.zeros((8, self.tk), F32))
        self.k2max_ref[...] = jnp.broadcast_to(jnp.max(best[0:1, :], axis=1, keepdims=True), self.k2max_ref.shape)

    def score_bound(self):
        q = self.masked_q()
        q2 = jnp.max(jnp.sum(q * q, axis=1, keepdims=True), axis=0, keepdims=True)
        return jnp.sqrt(q2 * self.k2max_ref[0:1, 0:1]) * 1.01 + 0.01

    def set_query(self, shift):
        q = self.masked_q()
        if shift is not None:
            lane = lax.broadcasted_iota(jnp.int32, q.shape, 1)
            q = jnp.where(lane == self.bias_lane, -shift, q)
        self.qz_ref[...] = q.astype(BF16)

    def _qk(self, blk):
        return lax.dot_general(self.k_ref[self._keys(blk), :], self.qz_ref[...], (((1,), (1,)), ((), ())),
                               preferred_element_type=F32)

    def _pv(self, blk, p):
        return jnp.dot(self.vt_ref[:, self._keys(blk)], p.astype(BF16), preferred_element_type=F32)

    def scores(self, blk, slot):
        s = self._qk(blk)
        self.bufs[slot][...] = s
        return jnp.max(s, axis=0, keepdims=True)

    def consume(self, blk, slot, blk_max, m, l):
        m_new = jnp.maximum(m, blk_max)
        p = jnp.exp2(self.bufs[slot][...] - m_new)
        alpha = jnp.exp2(m - m_new)
        l_new = alpha * l + jnp.sum(p, axis=0, keepdims=True)
        self.acc_ref[...] = alpha * self.acc_ref[...] + self._pv(blk, p)
        return m_new, l_new

    def accumulate(self, blk, part):
        p = jnp.exp2(self._qk(blk))
        self.acc_ref[...] += self._pv(blk, p)
        return part + jnp.sum(p.reshape(self.tk // 8, 8, p.shape[1]), axis=0)


def _flash_exact(streams, nblk, tq):
    m0 = jnp.full((1, tq), NEG_BIG, F32)
    l0 = jnp.zeros((1, tq), F32)
    for st in streams:
        st.set_query(None)
    first = tuple((m0, l0, st.scores(0, 0)) for st in streams)

    def pair(j, carry):
        b0 = 2 * j
        b2 = jnp.minimum(b0 + 2, nblk - 1)
        out = []
        for st, (m, l, max0) in zip(streams, carry):
            max1 = st.scores(b0 + 1, 1)
            m, l = st.consume(b0, 0, max0, m, l)
            max2 = st.scores(b2, 0)
            m, l = st.consume(b0 + 1, 1, max1, m, l)
            out.append((m, l, max2))
        return tuple(out)

    final = lax.fori_loop(0, nblk // 2, pair, first)
    for st, (_, l, _) in zip(streams, final):
        st.sum_ref[...] = l


def _flash_shifted(streams, bounds, nblk, tq):
    for st, bound in zip(streams, bounds):
        st.set_query(bound)

    unroll = SHIFT_UNROLL if nblk % SHIFT_UNROLL == 0 else 2

    def group(j, parts):
        for u in range(unroll):
            parts = tuple(st.accumulate(unroll * j + u, part) for st, part in zip(streams, parts))
        return parts

    parts = lax.fori_loop(0, nblk // unroll, group, tuple(jnp.zeros((8, tq), F32) for _ in streams))
    for st, part in zip(streams, parts):
        st.sum_ref[...] = jnp.sum(part, axis=0, keepdims=True)


def _flash(streams, nblk, tq, new_keys):
    assert nblk % 2 == 0

    @pl.when(new_keys)
    def _():
        for st in {id(st.k2max_ref): st for st in streams}.values():
            st.store_key_norm(nblk)

    for st in streams:
        st.acc_ref[...] = jnp.zeros(st.acc_ref.shape, F32)
    bounds = [st.score_bound() for st in streams]
    shift_ok = functools.reduce(jnp.maximum, bounds)[0, 0] <= SHIFT_MAX_BOUND

    @pl.when(shift_ok)
    def _():
        _flash_shifted(streams, bounds, nblk, tq)

    @pl.when(jnp.logical_not(shift_ok))
    def _():
        _flash_exact(streams, nblk, tq)


def _da_body(lam_ref, g_ref, m1_ref, m2_ref, q_ref, k1_ref, k2_ref, vt_ref, o_ref,
             acc_ref, sum_ref, qz_ref, k2max_ref, *bufs, tk, lambda_init):
    tq = q_ref.shape[0] // ATT_QSUB
    masks = (m1_ref[...], m2_ref[...])
    lam_scr = bufs[-1]
    new_head = pl.program_id(2) == 0

    @pl.when(new_head)
    def _():
        lq1, lk1, lq2, lk2 = (lam_ref[r:r + 1, :] for r in range(4))
        lam = (jnp.exp(jnp.sum(lq1 * lk1, axis=-1, keepdims=True))
               - jnp.exp(jnp.sum(lq2 * lk2, axis=-1, keepdims=True)) + lambda_init)
        lam_scr[...] = jnp.broadcast_to(lam, lam_scr.shape)

    def query_block(sub, carry):
        rows = pl.ds(pl.multiple_of(sub * tq, tq), tq)
        streams = [_Stream(q_ref.at[rows, :], masks[j], DA_BIAS_LANES[j], (k1_ref, k2_ref)[j], vt_ref, acc_ref.at[j],
                           sum_ref.at[j], qz_ref.at[j], k2max_ref.at[j], bufs[2 * j:2 * j + 2], tk) for j in range(2)]
        _flash(streams, k1_ref.shape[0] // tk, tq, jnp.logical_and(new_head, sub == 0))
        lam = lam_scr[0:1, 0:1]
        ot = acc_ref[0] / sum_ref[0] - lam * (acc_ref[1] / sum_ref[1])
        ot = ot * lax.rsqrt(jnp.mean(ot * ot, axis=0, keepdims=True) + EPS)
        o_ref[rows, :] = (ot.T * g_ref[...] * (1.0 - lambda_init)).astype(o_ref.dtype)
        return carry

    lax.fori_loop(0, ATT_QSUB, query_block, 0)


def _mla_body(q_ref, k_ref, vt_ref, o_ref, acc_ref, sum_ref, qz_ref, k2max_ref, *bufs, tk):
    nq = acc_ref.shape[0]
    tq = q_ref.shape[0] // (nq * ATT_QSUB)
    key_norm = k2max_ref.at[0]
    new_head = pl.program_id(2) == 0

    def query_blocks(sub, carry):
        rows = [pl.ds(pl.multiple_of((sub * nq + i) * tq, tq), tq) for i in range(nq)]
        streams = [_Stream(q_ref.at[rows[i], :], None, MLA_BIAS_LANE, k_ref, vt_ref, acc_ref.at[i], sum_ref.at[i],
                           qz_ref.at[i], key_norm, bufs[2 * i:2 * i + 2], tk) for i in range(nq)]
        _flash(streams, k_ref.shape[0] // tk, tq, jnp.logical_and(new_head, sub == 0))
        for i in range(nq):
            o_ref[rows[i], :] = (acc_ref[i] / sum_ref[i]).T.astype(o_ref.dtype)
        return carry

    lax.fori_loop(0, ATT_QSUB, query_blocks, 0)


def _attn_specs(batch, heads, seq, qk_w, nq, tq_max, tk_max):
    tq = min(tq_max, seq // nq)
    tk = min(tk_max, seq)
    q_spec = pl.BlockSpec((None, None, nq * tq, qk_w), lambda b, h, i: (b, h, i, 0))
    k_spec = pl.BlockSpec((None, None, seq, qk_w), lambda b, h, i: (b, h, 0, 0))
    vt_spec = pl.BlockSpec((None, None, LANES, seq), lambda b, h, i: (b, h, 0, 0))
    o_spec = pl.BlockSpec((None, nq * tq, LANES), lambda b, h, i: (b, i, h))
    o_shape = jax.ShapeDtypeStruct((batch, seq, heads * LANES), BF16)
    return tq, tk, (batch, heads, seq // (nq * tq)), q_spec, k_spec, vt_spec, o_spec, o_shape


def _attn_scratch(streams, key_sets, tq, tk, qk_w):
    return ([pltpu.VMEM((streams, LANES, tq), F32),
             pltpu.VMEM((streams, 1, tq), F32),
             pltpu.VMEM((streams, tq, qk_w), BF16),
             pltpu.VMEM((key_sets, 8, LANES), F32)]
            + [pltpu.VMEM((tk, tq), F32)] * (2 * streams))


def _da_attn(lam_vecs, subln_g, masks, q, k1, k2, vt, *, lambda_init):
    batch, heads, seq, _ = q.shape
    tq, tk, grid, q_spec, k_spec, vt_spec, o_spec, o_shape = _attn_specs(
        batch, heads, seq, LANES, ATT_QSUB, DA_TQ, DA_TK)
    lane_row = pl.BlockSpec((1, LANES), lambda b, h, i: (0, 0))
    return pl.pallas_call(
        functools.partial(_da_body, tk=tk, lambda_init=lambda_init),
        grid=grid,
        in_specs=[pl.BlockSpec(lam_vecs.shape, lambda b, h, i: (0, 0)), lane_row, lane_row, lane_row,
                  q_spec, k_spec, k_spec, vt_spec],
        out_specs=o_spec,
        out_shape=o_shape,
        scratch_shapes=_attn_scratch(2, 2, tq, tk, LANES) + [pltpu.VMEM((8, LANES), F32)],
        compiler_params=_params("arbitrary", "arbitrary", "arbitrary"),
        name="da_attn",
    )(lam_vecs, subln_g, *masks, q, k1, k2, vt)


def _mla_attn(q, k, vt):
    batch, heads, seq, qk_w = q.shape
    nq = MLA_QSTREAMS
    tq, tk, grid, q_spec, k_spec, vt_spec, o_spec, o_shape = _attn_specs(
        batch, heads, seq, qk_w, nq * ATT_QSUB, MLA_TQ, MLA_TK)
    return pl.pallas_call(
        functools.partial(_mla_body, tk=tk),
        grid=grid,
        in_specs=[q_spec, k_spec, vt_spec],
        out_specs=o_spec,
        out_shape=o_shape,
        scratch_shapes=_attn_scratch(nq, 1, tq, tk, qk_w),
        compiler_params=_params("arbitrary", "arbitrary", "arbitrary"),
        name="mla_attn",
    )(q, k, vt)


def _outproj_body(x_ref, mod_ref, oda_ref, oml_ref, wo_ref, o_ref):
    split = oda_ref.shape[1]
    mix = (jnp.dot(oda_ref[...], wo_ref[0:split, :], preferred_element_type=F32)
           + jnp.dot(oml_ref[...], wo_ref[split:, :], preferred_element_type=F32))
    o_ref[...] = x_ref[...] + mod_ref[5:6, :] * mix


def _outproj(x, mod, o_da, o_mla, wo, *, seq):
    t, d = x.shape
    tm = min(OUT_TM, seq)
    per_seq = seq // tm
    row = lambda i: (i, 0)
    return pl.pallas_call(
        _outproj_body,
        grid=(t // tm,),
        in_specs=[pl.BlockSpec((tm, d), row),
                  pl.BlockSpec((None, N_MOD, d), lambda i: (i // per_seq, 0, 0)),
                  pl.BlockSpec((tm, o_da.shape[1]), row),
                  pl.BlockSpec((tm, o_mla.shape[1]), row),
                  _const_spec(wo.shape)],
        out_specs=pl.BlockSpec((tm, d), row),
        out_shape=jax.ShapeDtypeStruct((t, d), F32),
        compiler_params=_params("arbitrary"),
        name="outproj",
    )(x, mod, o_da, o_mla, wo)


def _da_lane_layout():
    src = np.zeros(LANES, np.int32)
    rot = np.full(LANES, -1, np.int32)
    sign = np.zeros(LANES, np.float32)
    mask = np.zeros((2, LANES), np.float32)
    half = DA_ROT // 2
    for j in range(2):
        base = j * DA_QK_DIM
        for i in range(half):
            src[j * half + i] = base + i
            rot[j * half + i] = i
            sign[j * half + i] = -1.0
            src[64 + j * half + i] = base + half + i
            rot[64 + j * half + i] = i
            sign[64 + j * half + i] = 1.0
            mask[j, j * half + i] = mask[j, 64 + j * half + i] = 1.0
        plain = DA_QK_DIM - DA_ROT
        start = DA_ROT + j * 64
        for i in range(plain):
            src[start + i] = base + DA_ROT + i
            mask[j, start + i] = 1.0
    return src, rot, sign, mask


def _mla_lane_layout():
    src = np.full(LANES, -1, np.int32)
    rot = np.full(LANES, -1, np.int32)
    sign = np.zeros(LANES, np.float32)
    half = MLA_ROPE // 2
    for i in range(half):
        src[i], rot[i], sign[i] = i, i, -1.0
        src[64 + i], rot[64 + i], sign[64 + i] = half + i, i, 1.0
    return src, rot, sign


def _take_cols(w, src):
    src = [int(v) for v in src]
    parts, start = [], 0
    for i in range(1, len(src) + 1):
        same_run = i < len(src) and ((src[i] < 0 and src[i - 1] < 0) or (src[i - 1] >= 0 and src[i] == src[i - 1] + 1))
        if not same_run:
            width = i - start
            parts.append(jnp.zeros((w.shape[0], width), w.dtype) if src[start] < 0
                         else w[:, src[start]:src[start] + width])
            start = i
    return jnp.concatenate(parts, axis=1)


def _rope_tables(seq, dim, rot, sign):
    inv = ROPE_THETA ** (-jnp.arange(0, dim, 2, dtype=F32) / dim)
    ang = jnp.arange(seq, dtype=F32)[:, None] * _take_cols(inv[None, :], rot)
    return jnp.cos(ang), jnp.sin(ang) * jnp.asarray(sign)


def _prep_layer(l, ffn1_w1, ffn1_w3, ffn1_w2, w_in, mla_w_uq, mla_w_ukv, w_o, ffn2_w1, ffn2_w3, ffn2_w2):
    da_src, _, _, _ = _da_lane_layout()
    ml_src, _, _ = _mla_lane_layout()
    head_src = np.concatenate([h * 2 * DA_QK_DIM + da_src for h in range(DA_HEADS)])
    o1, o2, o3 = DA_Q_W, 2 * DA_Q_W, 2 * DA_Q_W + DA_V_W
    o5 = o3 + MLA_Q_RANK + MLA_KV_RANK
    wi = w_in[l].astype(BF16)
    win = jnp.concatenate([_take_cols(wi[:, :o1], head_src),
                           _take_cols(wi[:, o1:o2], head_src),
                           wi[:, o2:o5],
                           _take_cols(wi[:, o5:], ml_src)], axis=1)
    per_q = MLA_NOPE + MLA_ROPE
    uq_src = np.concatenate([np.concatenate([h * per_q + np.arange(MLA_NOPE),
                                             np.where(ml_src >= 0, h * per_q + MLA_NOPE + ml_src, -1)])
                             for h in range(MLA_HEADS)])
    per_kv = MLA_NOPE + MLA_V
    ukv_src = np.concatenate([h * per_kv + np.arange(MLA_NOPE) for h in range(MLA_HEADS)]
                             + [h * per_kv + MLA_NOPE + np.arange(MLA_V) for h in range(MLA_HEADS)])
    return dict(
        f1=(ffn1_w1[l].astype(BF16), ffn1_w3[l].astype(BF16), ffn1_w2[l].astype(BF16)),
        f2=(ffn2_w1[l].astype(BF16), ffn2_w3[l].astype(BF16), ffn2_w2[l].astype(BF16)),
        win=win,
        wuq=_take_cols(mla_w_uq[l].astype(BF16), uq_src),
        wukv=_take_cols(mla_w_ukv[l].astype(BF16), ukv_src),
        wo=w_o[l].astype(BF16),
    )


def _encode_group(x, mods, prepped, small, final_norm_g, tabs):
    batch, seq, d = x.shape
    xt = x.reshape(batch * seq, d)
    da_mask = _da_lane_layout()[3]
    masks = (jnp.asarray(da_mask[0:1]), jnp.asarray(da_mask[1:2]))
    depth = len(prepped)
    row = lambda v: v.reshape(1, -1)
    for l in range(depth):
        p, s, mod = prepped[l], small[l], mods[l]
        lambda_init = 0.8 - 0.6 * math.exp(-0.3 * l)
        xt = _ffn(xt, mod, row(s["ffn1_norm"]), *p["f1"], row(s["ffn1_norm"]),
                  seq=seq, mod_base=0, final_norm=False)
        qda, k1, k2, vtda, qml, kml, vtml = _inproj(
            xt, mod, row(s["attn_norm"]), p["win"], tabs, masks, row(s["mla_q_norm"]), row(s["mla_kv_norm"]),
            p["wuq"], p["wukv"], batch=batch, seq=seq)
        o_da = _da_attn(s["lam_vecs"], row(s["da_subln"]), masks, qda, k1, k2, vtda, lambda_init=lambda_init)
        o_mla = _mla_attn(qml, kml, vtml)
        xt = _outproj(xt, mod, o_da.reshape(batch * seq, -1), o_mla.reshape(batch * seq, -1),
                      p["wo"], seq=seq)
        last = l == depth - 1
        xt = _ffn(xt, mod, row(s["ffn2_norm"]), *p["f2"], row(final_norm_g),
                  seq=seq, mod_base=6, final_norm=last)
    return xt.reshape(batch, seq, d)


def kernel(x_prompt, x_sample, c_prompt, c_sample, ffn1_norm, ffn1_w1, ffn1_w3, ffn1_w2, attn_norm, w_in,
           da_lambda_q1, da_lambda_k1, da_lambda_q2, da_lambda_k2, da_subln, mla_q_norm, mla_w_uq, mla_kv_norm,
           mla_w_ukv, w_o, ffn2_norm, ffn2_w1, ffn2_w3, ffn2_w2, w_ada, b_ada, final_norm):
    depth = w_in.shape[0]
    nb_p, nb_s = c_prompt.shape[0], c_sample.shape[0]
    c_all = jnp.concatenate([c_prompt, c_sample], axis=0)
    rows = -(-c_all.shape[0] // 8) * 8
    c_pad = jnp.pad(c_all, ((0, rows - c_all.shape[0]), (0, 0)))

    prepped, small, mods_p, mods_s = [], [], [], []
    for l in range(depth):
        prepped.append(_prep_layer(l, ffn1_w1, ffn1_w3, ffn1_w2, w_in, mla_w_uq, mla_w_ukv, w_o,
                                   ffn2_w1, ffn2_w3, ffn2_w2))
        small.append(dict(
            ffn1_norm=ffn1_norm[l], attn_norm=attn_norm[l], ffn2_norm=ffn2_norm[l], da_subln=da_subln[l],
            mla_q_norm=mla_q_norm[l], mla_kv_norm=mla_kv_norm[l],
            lam_vecs=jnp.stack([da_lambda_q1[l], da_lambda_k1[l], da_lambda_q2[l], da_lambda_k2[l]])))
        m = _ada(c_pad, w_ada[l], b_ada[l].reshape(1, -1))
        mods_p.append(m[:nb_p].reshape(nb_p, N_MOD, D_MODEL))
        mods_s.append(m[nb_p:nb_p + nb_s].reshape(nb_s, N_MOD, D_MODEL))

    _, da_rot, da_sign, _ = _da_lane_layout()
    _, ml_rot, ml_sign = _mla_lane_layout()
    max_seq = max(x_prompt.shape[1], x_sample.shape[1])
    tabs = _rope_tables(max_seq, DA_ROT, da_rot, da_sign) + _rope_tables(max_seq, MLA_ROPE, ml_rot, ml_sign)

    y_prompt = _encode_group(x_prompt, mods_p, prepped, small, final_norm, tabs)
    y_sample = _encode_group(x_sample, mods_s, prepped, small, final_norm, tabs)
    return (y_prompt, y_sample)
```
